```python
import math
import jax
import jax.numpy as jnp
from jax import lax
import numpy as np

D_MODEL = 1024
BATCH = 16
SEQ = 2048
DEPTH = 2

GRID_W = 64
CTX_LEN = 256
N_EVEN = (DEPTH + 1) // 2
N_ODD = DEPTH // 2
EPS = 1e-6

GLA_HEADS = 4
GLA_KEY_W = D_MODEL // 2
GLA_VAL_W = D_MODEL
GLA_KEY_DIM = GLA_KEY_W // GLA_HEADS
GLA_VAL_DIM = GLA_VAL_W // GLA_HEADS
GLA_GATE_RANK = 16
GLA_GATE_NORM = 16.0
HG_DIM = 128
HG_W = D_MODEL
HG_HEADS = HG_W // HG_DIM
LIN_CHUNK = 32
ATT_HEAD_DIM = 64
ATT_Q_HEADS = D_MODEL // ATT_HEAD_DIM
ATT_KV_HEADS = ATT_Q_HEADS // 4
ATT_GROUP = ATT_Q_HEADS // ATT_KV_HEADS
ATT_Q_W = ATT_Q_HEADS * ATT_HEAD_DIM
ATT_KV_W = ATT_KV_HEADS * ATT_HEAD_DIM
WINDOW = 128
ATT_BLOCK = 128
ROPE_BASE = 10000.0
NEG_INF = -1e30
S5_W = D_MODEL
S5_GROUP_CH = 16
S5_GROUPS = S5_W // S5_GROUP_CH
S5_STATE = 64

EV_SPLITS = (GLA_KEY_W, GLA_KEY_W, GLA_VAL_W, 2 * GLA_GATE_RANK, GLA_VAL_W, HG_W, 2 * HG_W, HG_W, HG_W)
EV_IN_W = sum(EV_SPLITS)
EV_OUT_W = GLA_VAL_W + HG_W
OD_SPLITS = (ATT_Q_W, ATT_KV_W, ATT_KV_W, ATT_Q_W, S5_W, S5_W)
OD_IN_W = sum(OD_SPLITS)
OD_OUT_W = ATT_Q_W + S5_W

kernel_name = 'hybrid_prefix_dit_block'


def rmsnorm(x, g):
    xf = x.astype(jnp.float32)
    return xf * lax.rsqrt(jnp.mean(xf * xf, axis=-1, keepdims=True) + EPS) * g.astype(jnp.float32)


def split_cols(p, sizes):
    return jnp.split(p, [int(i) for i in np.cumsum(sizes)[:-1]], axis=-1)


def to_heads(a, nh):
    bsz, length, width = a.shape
    return a.reshape(bsz, length, nh, width // nh).transpose(0, 2, 1, 3)


def from_heads(a):
    bsz, nh, length, d = a.shape
    return a.transpose(0, 2, 1, 3).reshape(bsz, length, nh * d)


def chunk_recurrence(q, k, v, g, s0):
    bsz, nh, length, dk = q.shape
    dv = v.shape[-1]
    n = length // LIN_CHUNK
    q, k, v, g = (a.astype(jnp.float32).reshape(bsz, nh, n, LIN_CHUNK, a.shape[-1]) for a in (q, k, v, g))
    bcum = jnp.cumsum(g, axis=3)
    blast = bcum[:, :, :, -1:, :]
    q_dec = q * jnp.exp(bcum)
    k_inv = k * jnp.exp(-bcum)
    k_tail = k * jnp.exp(blast - bcum)
    lower = jnp.tril(jnp.ones((LIN_CHUNK, LIN_CHUNK), dtype=bool))
    att = jnp.where(lower, jnp.einsum('bhncd,bhnsd->bhncs', q_dec, k_inv), 0.0)
    o_intra = jnp.einsum('bhncs,bhnsv->bhncv', att, v)

    def step(state, inp):
        q_n, k_n, v_n, dec_n = inp
        o_n = jnp.einsum('bhcd,bhdv->bhcv', q_n, state)
        state = dec_n[..., None] * state + jnp.einsum('bhcd,bhcv->bhdv', k_n, v_n)
        return state, o_n

    xs = (jnp.moveaxis(q_dec, 2, 0), jnp.moveaxis(k_tail, 2, 0), jnp.moveaxis(v, 2, 0),
          jnp.moveaxis(jnp.exp(blast[:, :, :, 0, :]), 2, 0))
    s_final, o_inter = lax.scan(step, s0.astype(jnp.float32), xs)
    o = o_intra + jnp.moveaxis(o_inter, 0, 2)
    return o.reshape(bsz, nh, length, dv), s_final


def bidirectional_scan(c_feats, l_feats, need_ctx):
    q_c, k_c, v_c, g_c = c_feats
    q_l, k_l, v_l, g_l = l_feats
    bsz, nh, _, dk = q_l.shape
    dv = v_l.shape[-1]
    zero = jnp.zeros((bsz, nh, dk, dv), jnp.float32)

    def rev(a):
        return jnp.flip(a, axis=2)

    o_cf, s_f = chunk_recurrence(q_c, k_c[0], v_c, g_c[0], zero)
    o_lf, _ = chunk_recurrence(q_l, k_l[0], v_l, g_l[0], s_f)
    o_cb, s_b = chunk_recurrence(rev(q_c), rev(k_c[1]), rev(v_c), rev(g_c[1]), zero)
    o_lb, _ = chunk_recurrence(rev(q_l), rev(k_l[1]), rev(v_l), rev(g_l[1]), s_b)
    o_c = o_cf + rev(o_cb) if need_ctx else None
    return o_c, o_lf + rev(o_lb)


def even_features(h, w_in, gk_w, gk_b, lb):
    aq, ak, av, alr, agate, bq, bf, bi, bgate = split_cols(h @ w_in, EV_SPLITS)
    lr_pair = jnp.split(alr, 2, axis=-1)
    g_gla = tuple(to_heads(jax.nn.log_sigmoid((lr_pair[d] @ gk_w[d] + gk_b[d]).astype(jnp.float32)) / GLA_GATE_NORM,
                           GLA_HEADS) for d in range(2))
    f_pair = jnp.split(bf, 2, axis=-1)
    forget = tuple(lb[d] + (1.0 - lb[d]) * jax.nn.sigmoid(f_pair[d].astype(jnp.float32)) for d in range(2))
    ak_h = to_heads(ak, GLA_HEADS)
    return {
        'gla': (to_heads(aq, GLA_HEADS) * GLA_KEY_DIM ** -0.5, (ak_h, ak_h), to_heads(av, GLA_HEADS), g_gla),
        'hgrn': (to_heads(bq, HG_HEADS), tuple(to_heads(1.0 - f, HG_HEADS) for f in forget),
                 to_heads(bi, HG_HEADS), tuple(to_heads(jnp.log(f), HG_HEADS) for f in forget)),
        'gla_gate': agate,
        'hgrn_gate': bgate,
    }


def even_out(o_gla, gate_a, o_hg, gate_b, gla_g, hg_g, w_out):
    a = from_heads(rmsnorm(o_gla, gla_g)) * jax.nn.silu(gate_a)
    b = from_heads(rmsnorm(o_hg, hg_g)) * jax.nn.silu(gate_b)
    return jnp.concatenate([a, b], axis=-1) @ w_out


def even_mixer(h_c, h_l, w_in, w_out, gk_w, gk_b, gla_g, lb, hg_g, need_ctx):
    f_c = even_features(h_c, w_in, gk_w, gk_b, lb)
    f_l = even_features(h_l, w_in, gk_w, gk_b, lb)
    gla_c, gla_l = bidirectional_scan(f_c['gla'], f_l['gla'], need_ctx)
    hg_c, hg_l = bidirectional_scan(f_c['hgrn'], f_l['hgrn'], need_ctx)
    y_l = even_out(gla_l, f_l['gla_gate'], hg_l, f_l['hgrn_gate'], gla_g, hg_g, w_out)
    y_c = even_out(gla_c, f_c['gla_gate'], hg_c, f_c['hgrn_gate'], gla_g, hg_g, w_out) if need_ctx else None
    return y_c, y_l


def axial_rope(x, row, col):
    half = ATT_HEAD_DIM // 2
    quarter = half // 2
    freqs = ROPE_BASE ** (-jnp.arange(quarter, dtype=jnp.float32) / quarter)

    def rot(xa, pos):
        ang = pos.astype(jnp.float32)[:, None] * freqs
        cos, sin = jnp.cos(ang), jnp.sin(ang)
        x1, x2 = xa[..., :quarter], xa[..., quarter:]
        return jnp.concatenate([x1 * cos - x2 * sin, x2 * cos + x1 * sin], axis=-1)

    return jnp.concatenate([rot(x[..., :half], row), rot(x[..., half:], col)], axis=-1)


def window_attention(q_c, k_c, v_c, q_l, k_l, v_l, sink, need_ctx):
    bsz, hkv, grp, length, d = q_l.shape
    lc = k_c.shape[2]
    nb = length // ATT_BLOCK
    sink = sink.astype(jnp.float32).reshape(hkv, grp)
    pad = ((0, 0), (0, 0), (ATT_BLOCK, ATT_BLOCK), (0, 0))
    kpad, vpad = jnp.pad(k_l, pad), jnp.pad(v_l, pad)

    def band(a):
        return jnp.concatenate([a[:, :, s * ATT_BLOCK: s * ATT_BLOCK + length].reshape(bsz, hkv, nb, ATT_BLOCK, d)
                                for s in range(3)], axis=3)

    kb, vb = band(kpad), band(vpad)
    qb = q_l.reshape(bsz, hkv, grp, nb, ATT_BLOCK, d)
    blk = jnp.arange(nb)[:, None, None]
    qpos = blk * ATT_BLOCK + jnp.arange(ATT_BLOCK)[None, :, None]
    kpos = (blk - 1) * ATT_BLOCK + jnp.arange(3 * ATT_BLOCK)[None, None, :]
    valid = (jnp.abs(qpos - kpos) <= WINDOW) & (kpos >= 0) & (kpos < length)
    s_band = jnp.where(valid, jnp.einsum('bhgnqd,bhnkd->bhgnqk', qb, kb).astype(jnp.float32), NEG_INF)
    s_ctx = jnp.einsum('bhgnqd,bhkd->bhgnqk', qb, k_c).astype(jnp.float32)
    s_sink = jnp.broadcast_to(sink[None, :, :, None, None, None], s_ctx.shape[:-1] + (1,))
    p = jax.nn.softmax(jnp.concatenate([s_ctx, s_band, s_sink], axis=-1), axis=-1)
    o_l = (jnp.einsum('bhgnqk,bhkd->bhgnqd', p[..., :lc], v_c)
           + jnp.einsum('bhgnqk,bhnkd->bhgnqd', p[..., lc:lc + 3 * ATT_BLOCK], vb))
    o_l = o_l.reshape(bsz, hkv, grp, length, d).transpose(0, 3, 1, 2, 4).reshape(bsz, length, ATT_Q_W)
    o_c = None
    if need_ctx:
        s_c = jnp.einsum('bhgqd,bhkd->bhgqk', q_c, k_c).astype(jnp.float32)
        sink_c = jnp.broadcast_to(sink[None, :, :, None, None], s_c.shape[:-1] + (1,))
        p_c = jax.nn.softmax(jnp.concatenate([s_c, sink_c], axis=-1), axis=-1)
        o_c = jnp.einsum('bhgqk,bhkd->bhgqd', p_c[..., :lc], v_c)
        o_c = o_c.transpose(0, 3, 1, 2, 4).reshape(bsz, lc, ATT_Q_W)
    return o_c, o_l


def _combine(e1, e2):
    a1, b1 = e1
    a2, b2 = e2
    return a1 * a2, a2 * b1 + b2


def diag_scan(abar, drive, s0):
    if s0 is not None:
        drive = drive.at[0].add(abar * s0)
    a = jnp.broadcast_to(abar, (drive.shape[0], 1) + abar.shape)
    _, states = lax.associative_scan(_combine, (a, drive), axis=0)
    return states


def s5_bidirectional(u_c, u_l, lam_re, lam_im, log_dt, b_re, b_im, c_re, c_im, d_skip, need_ctx):
    f32 = jnp.float32
    bmat = lax.complex(b_re.astype(f32), b_im.astype(f32))
    cmat = lax.complex(c_re.astype(f32), c_im.astype(f32))
    uc_cx, ul_cx = u_c.astype(jnp.complex64), u_l.astype(jnp.complex64)

    def read(states):
        return jnp.real(jnp.einsum('lbgp,ghp->blgh', states, cmat))

    y_c = d_skip.astype(f32) * u_c.astype(f32) if need_ctx else None
    y_l = d_skip.astype(f32) * u_l.astype(f32)
    for direction in range(2):
        lam = lax.complex(lam_re[direction].astype(f32), lam_im[direction].astype(f32))
        dt = jnp.exp(log_dt[direction].astype(f32))[:, None]
        abar = jnp.exp(lam * dt)
        bbar = ((abar - 1.0) / lam)[:, :, None] * bmat
        order = (lambda a: jnp.flip(a, axis=0)) if direction == 1 else (lambda a: a)
        st_c = diag_scan(abar, order(jnp.einsum('blgh,gph->lbgp', uc_cx, bbar)), None)
        st_l = order(diag_scan(abar, order(jnp.einsum('blgh,gph->lbgp', ul_cx, bbar)), st_c[-1]))
        y_l = y_l + read(st_l)
        if need_ctx:
            y_c = y_c + read(order(st_c))
    return y_c, y_l


def odd_features(h, w_in):
    q, k, v, g_att, u, g_s5 = split_cols(h @ w_in, OD_SPLITS)
    bsz, length, _ = h.shape
    q = q.reshape(bsz, length, ATT_KV_HEADS, ATT_GROUP, ATT_HEAD_DIM).transpose(0, 2, 3, 1, 4) * ATT_HEAD_DIM ** -0.5
    u = u.reshape(bsz, length, S5_GROUPS, S5_GROUP_CH)
    return q, to_heads(k, ATT_KV_HEADS), to_heads(v, ATT_KV_HEADS), g_att, u, g_s5


def odd_out(o_att, g_att, y_s5, g_s5, glu_w, w_out):
    z = jax.nn.gelu(y_s5.reshape(y_s5.shape[0], y_s5.shape[1], S5_W))
    a, b = jnp.split(z @ glu_w, 2, axis=-1)
    s5 = a * jax.nn.sigmoid(b)
    return jnp.concatenate([o_att * jax.nn.silu(g_att), s5 * jax.nn.silu(g_s5)], axis=-1) @ w_out


def odd_mixer(h_c, h_l, w_in, w_out, sink, lam_re, lam_im, log_dt, b_re, b_im, c_re, c_im, d_skip, glu_w,
              row, col, need_ctx):
    q_l, k_l, v_l, ga_l, u_l, gs_l = odd_features(h_l, w_in)
    q_c, k_c, v_c, ga_c, u_c, gs_c = odd_features(h_c, w_in)
    q_l, k_l = axial_rope(q_l, row, col), axial_rope(k_l, row, col)
    att_c, att_l = window_attention(q_c, k_c, v_c, q_l, k_l, v_l, sink, need_ctx)
    ssm_c, ssm_l = s5_bidirectional(u_c, u_l, lam_re, lam_im, log_dt, b_re, b_im, c_re, c_im, d_skip, need_ctx)
    y_l = odd_out(att_l, ga_l, ssm_l, gs_l, glu_w, w_out)
    y_c = odd_out(att_c, ga_c, ssm_c, gs_c, glu_w, w_out) if need_ctx else None
    return y_c, y_l


def setup_inputs(seed: int = 0) -> dict:
    key = jax.random.key(seed)
    ks = jax.random.split(key, 27)
    f32 = jnp.float32

    def nrm(k, shape, scale):
        return jax.random.normal(k, shape, f32) * scale

    d = D_MODEL
    g5, p5, h5 = S5_GROUPS, S5_STATE, S5_GROUP_CH
    return {
        'x': nrm(ks[0], (BATCH, SEQ, d), 1.0),
        'c': nrm(ks[1], (BATCH, d), 1.0),
        'ctx': nrm(ks[2], (BATCH, CTX_LEN, d), 1.0),
        'c_ctx': nrm(ks[3], (d,), 1.0),
        'ada_w': nrm(ks[4], (DEPTH, d, 3 * d), 0.5 * d ** -0.5),
        'ada_b': nrm(ks[5], (DEPTH, 3 * d), 0.02),
        'norm_g': 1.0 + nrm(ks[6], (DEPTH, d), 0.05),
        'final_norm_g': 1.0 + nrm(ks[7], (d,), 0.05),
        'ev_w_in': nrm(ks[8], (N_EVEN, d, EV_IN_W), d ** -0.5),
        'ev_w_out': nrm(ks[9], (N_EVEN, EV_OUT_W, d), EV_OUT_W ** -0.5),
        'gla_gk_w': nrm(ks[10], (N_EVEN, 2, GLA_GATE_RANK, GLA_KEY_W), GLA_GATE_RANK ** -0.5),
        'gla_gk_b': nrm(ks[11], (N_EVEN, 2, GLA_KEY_W), 0.1),
        'gla_norm_g': 1.0 + nrm(ks[12], (N_EVEN, GLA_VAL_DIM), 0.05),
        'hgrn_lb_raw': nrm(ks[13], (2, N_EVEN + 1, HG_W), 0.5),
        'hgrn_norm_g': 1.0 + nrm(ks[14], (N_EVEN, HG_DIM), 0.05),
        'od_w_in': nrm(ks[15], (N_ODD, d, OD_IN_W), d ** -0.5),
        'od_w_out': nrm(ks[16], (N_ODD, OD_OUT_W, d), OD_OUT_W ** -0.5),
        'attn_sink': nrm(ks[17], (N_ODD, ATT_Q_HEADS), 0.5),
        's5_lambda_re': -0.5 + nrm(ks[18], (N_ODD, 2, g5, p5), 0.01),
        's5_lambda_im': math.pi * jnp.arange(p5, dtype=f32) + nrm(ks[19], (N_ODD, 2, g5, p5), 0.01),
        's5_log_dt': jax.random.uniform(ks[20], (N_ODD, 2, g5), f32, math.log(1e-3), math.log(1e-1)),
        's5_b_re': nrm(ks[21], (N_ODD, g5, p5, h5), (2 * h5) ** -0.5),
        's5_b_im': nrm(ks[22], (N_ODD, g5, p5, h5), (2 * h5) ** -0.5),
        's5_c_re': nrm(ks[23], (N_ODD, g5, h5, p5), p5 ** -0.5),
        's5_c_im': nrm(ks[24], (N_ODD, g5, h5, p5), p5 ** -0.5),
        's5_d': nrm(ks[25], (N_ODD, g5, h5), 1.0),
        's5_glu_w': nrm(ks[26], (N_ODD, S5_W, 2 * S5_W), S5_W ** -0.5),
    }


def reference(x, c, ctx, c_ctx, ada_w, ada_b, norm_g, final_norm_g,
              ev_w_in, ev_w_out, gla_gk_w, gla_gk_b, gla_norm_g, hgrn_lb_raw, hgrn_norm_g,
              od_w_in, od_w_out, attn_sink, s5_lambda_re, s5_lambda_im, s5_log_dt,
              s5_b_re, s5_b_im, s5_c_re, s5_c_im, s5_d, s5_glu_w):
    length = x.shape[1]
    rows = length // GRID_W
    row = jnp.repeat(jnp.arange(rows), GRID_W)
    col = jnp.tile(jnp.arange(GRID_W), rows)
    lb_all = jnp.cumsum(jax.nn.softmax(hgrn_lb_raw.astype(jnp.float32), axis=1), axis=1)
    xl, xc = x, ctx
    for layer in range(DEPTH):
        need_ctx = layer < DEPTH - 1
        w_ada, b_ada = ada_w[layer], ada_b[layer]
        shift_l, scale_l, gate_l = jnp.split(jax.nn.silu(c) @ w_ada + b_ada, 3, axis=-1)
        shift_c, scale_c, gate_c = jnp.split(jax.nn.silu(c_ctx) @ w_ada + b_ada, 3, axis=-1)
        h_l = rmsnorm(xl, norm_g[layer]) * (1.0 + scale_l[:, None, :]) + shift_l[:, None, :]
        h_c = rmsnorm(xc, norm_g[layer]) * (1.0 + scale_c) + shift_c
        idx = layer // 2
        if layer % 2 == 0:
            y_c, y_l = even_mixer(h_c, h_l, ev_w_in[idx], ev_w_out[idx], gla_gk_w[idx], gla_gk_b[idx],
                                  gla_norm_g[idx], lb_all[:, idx], hgrn_norm_g[idx], need_ctx)
        else:
            y_c, y_l = odd_mixer(h_c, h_l, od_w_in[idx], od_w_out[idx], attn_sink[idx],
                                 s5_lambda_re[idx], s5_lambda_im[idx], s5_log_dt[idx],
                                 s5_b_re[idx], s5_b_im[idx], s5_c_re[idx], s5_c_im[idx], s5_d[idx],
                                 s5_glu_w[idx], row, col, need_ctx)
        xl = xl + gate_l[:, None, :] * y_l
        if need_ctx:
            xc = xc + gate_c * y_c
    return rmsnorm(xl, final_norm_g)
```

```python
import functools
import math

import jax
import jax.numpy as jnp
import numpy as np
from jax import lax
from jax.experimental import pallas as pl
from jax.experimental.pallas import tpu as pltpu

EPS = 1e-6
LANES = 128
ROW_TILE = 256
LIN_CHUNK = 32
CHUNKS_PER_BLOCK = ROW_TILE // LIN_CHUNK
VMEM_LIMIT = 56 * 1024 * 1024

GLA_HEADS = 4
GLA_GATE_RANK = 16
GLA_GATE_NORM = 16.0
HG_DIM = 128
ATT_HEAD_DIM = 64
ATT_GROUP = 4
WINDOW = 128
ATT_BLOCK = 128
ROPE_BASE = 10000.0
NEG_INF = -1e30
S5_GROUP_CH = 16
S5_STATE = 64
S5_CHUNK = 16
S5_GPB = LANES // S5_GROUP_CH
S5_FOLD = S5_CHUNK * LANES

F32 = jnp.float32
BF16 = jnp.bfloat16
HIGHEST = lax.Precision.HIGHEST


def _cparams(*sem):
    return pltpu.CompilerParams(dimension_semantics=sem, vmem_limit_bytes=VMEM_LIMIT)


def _dot(a, b):
    return jnp.dot(a, b, preferred_element_type=F32)


def _dot_nt(a, b):
    return lax.dot_general(a, b, (((1,), (1,)), ((), ())), preferred_element_type=F32)


def _dot_tn(a, b):
    return lax.dot_general(a, b, (((0,), (0,)), ((), ())), preferred_element_type=F32)


def _silu(v):
    return v * jax.nn.sigmoid(v)


def _split_hi_lo(v):
    hi = v.astype(BF16)
    lo = (v - hi.astype(F32)).astype(BF16)
    return hi, lo


def _ada_kernel(c_ref, w_ref, b_ref, o_ref):
    o_ref[0] = _dot(_silu(c_ref[...]).astype(BF16), w_ref[0]) + b_ref[0]


def _ada_mods(cc, ada_w, ada_b, tn=512):
    depth, d, n = ada_w.shape
    rows = cc.shape[0]
    return pl.pallas_call(
        _ada_kernel,
        grid=(depth, n // tn),
        in_specs=[pl.BlockSpec((rows, d), lambda l, j: (0, 0)),
                  pl.BlockSpec((1, d, tn), lambda l, j: (l, 0, j)),
                  pl.BlockSpec((1, 1, tn), lambda l, j: (l, 0, j))],
        out_specs=pl.BlockSpec((1, rows, tn), lambda l, j: (l, 0, j)),
        out_shape=jax.ShapeDtypeStruct((depth, rows, n), F32),
        compiler_params=_cparams("arbitrary", "arbitrary"),
        name="ada_mods",
    )(cc, ada_w.astype(BF16), ada_b.reshape(depth, 1, n))


def _inproj_kernel(x_ref, mod_ref, g_ref, w_ref, o_ref, h_scr, *, lc):
    n_tiles = x_ref.shape[1] // ROW_TILE

    @pl.when(pl.program_id(1) == 0)
    def _():
        def body(i, carry):
            r0 = pl.multiple_of(i * ROW_TILE, ROW_TILE)
            which = (r0 >= lc).astype(jnp.int32)
            xf = x_ref[0, pl.ds(r0, ROW_TILE), :]
            ms = jnp.mean(xf * xf, axis=-1, keepdims=True)
            shift = mod_ref[0, which, 0:1, :]
            scale = mod_ref[0, which, 1:2, :]
            h = xf * lax.rsqrt(ms + EPS) * g_ref[...] * (1.0 + scale) + shift
            h_scr[pl.ds(r0, ROW_TILE), :] = h.astype(BF16)
            return carry
        lax.fori_loop(0, n_tiles, body, 0)

    for m in range(n_tiles):
        rows = slice(m * ROW_TILE, (m + 1) * ROW_TILE)
        o_ref[0, rows, :] = _dot(h_scr[rows, :], w_ref[...]).astype(o_ref.dtype)


def _inproj(xs, mods, g, w, out_dtype, lc, tn):
    b, t, d = xs.shape
    n = w.shape[1]
    assert n % tn == 0 and t % ROW_TILE == 0 and lc % ROW_TILE == 0
    return pl.pallas_call(
        functools.partial(_inproj_kernel, lc=lc),
        grid=(b, n // tn),
        in_specs=[pl.BlockSpec((1, t, d), lambda i, j: (i, 0, 0)),
                  pl.BlockSpec((1, 2, 3, d), lambda i, j: (i, 0, 0, 0)),
                  pl.BlockSpec((1, d), lambda i, j: (0, 0)),
                  pl.BlockSpec((d, tn), lambda i, j: (0, j))],
        out_specs=pl.BlockSpec((1, t, tn), lambda i, j: (i, 0, j)),
        out_shape=jax.ShapeDtypeStruct((b, t, n), out_dtype),
        scratch_shapes=[pltpu.VMEM((t, d), BF16)],
        compiler_params=_cparams("arbitrary", "arbitrary"),
        name="inproj",
    )(xs, mods, g.reshape(1, d), w)


def _scan_direction(d, q_of, k_of, v_of, g_of, tri_ref, sel_ref, o_scr, *, n_ctx_blk, n_blk, dk, dv):
    tri = tri_ref[d]
    lhs = jnp.concatenate([tri, tri_ref[2]], axis=0)
    mask = tri > 0
    sel = sel_ref[...]

    def block(i, s):
        if d == 0:
            blk = i
        else:
            blk = jnp.where(i < n_ctx_blk, n_ctx_blk - 1 - i, n_blk - 1 - (i - n_ctx_blk))
        rows = pl.ds(pl.multiple_of(blk * ROW_TILE, ROW_TILE), ROW_TILE)
        g = g_of(rows)
        q = q_of(rows)
        k = k_of(rows)
        v = v_of(rows)
        g_hi, g_lo = _split_hi_lo(g)
        cs = _dot(lhs, jnp.concatenate([g_hi, g_lo], axis=1))
        cs = cs[:, :dk] + cs[:, dk:]
        bcum, total = cs[:ROW_TILE], cs[ROW_TILE:]
        q_dec = (q * jnp.exp(bcum)).astype(BF16)
        k_inv = (k * jnp.exp(-bcum)).astype(BF16)
        k_tail = (k * jnp.exp(total - bcum)).astype(BF16)
        att = jnp.where(mask, _dot_nt(q_dec, k_inv), 0.0).astype(BF16)
        o_blk = _dot(att, v)
        dec_t = jnp.exp(_dot_tn(g_hi, sel) + _dot_tn(g_lo, sel))
        pieces = [None] * CHUNKS_PER_BLOCK
        order = range(CHUNKS_PER_BLOCK) if d == 0 else range(CHUNKS_PER_BLOCK - 1, -1, -1)
        for j in order:
            cr = slice(j * LIN_CHUNK, (j + 1) * LIN_CHUNK)
            pieces[j] = _dot(q_dec[cr], s.astype(BF16))
            s = s * dec_t[:, j:j + 1] + _dot_tn(k_tail[cr], v[cr])
        o_blk = o_blk + jnp.concatenate(pieces, axis=0)
        if d == 0:
            o_scr[rows, :] = o_blk
        else:
            o_scr[rows, :] += o_blk
        return s

    lax.fori_loop(0, n_blk, block, jnp.zeros((dk, dv), F32))


def _gated_norm_store(o_scr, gate_ref, gn_ref, out_ref, n_blk):
    def body(i, carry):
        rows = pl.ds(pl.multiple_of(i * ROW_TILE, ROW_TILE), ROW_TILE)
        o = o_scr[rows, :]
        ms = jnp.mean(o * o, axis=-1, keepdims=True)
        gate = gate_ref[0, rows, :].astype(F32)
        out_ref[0, rows, :] = (o * lax.rsqrt(ms + EPS) * gn_ref[...] * _silu(gate)).astype(out_ref.dtype)
        return carry
    lax.fori_loop(0, n_blk, body, 0)


def _log_sigmoid(v):
    return jnp.minimum(v, 0.0) - jnp.log1p(jnp.exp(-jnp.abs(v)))


def _gla_kernel(q_ref, k_ref, v_ref, gate_ref, lr_ref, gkw_ref, gkb_ref, gn_ref, tri_ref, sel_ref,
                out_ref, g_scr, o_scr, *, n_ctx_blk):
    t_rows, dk = q_ref.shape[1], q_ref.shape[2]
    dv = v_ref.shape[2]
    n_blk = t_rows // ROW_TILE
    scale = dk ** -0.5

    def gates(i, carry):
        rows = pl.ds(pl.multiple_of(i * ROW_TILE, ROW_TILE), ROW_TILE)
        lr = lr_ref[0, rows, :]
        for d in range(2):
            lr_d = lr[:, d * GLA_GATE_RANK:(d + 1) * GLA_GATE_RANK].astype(BF16)
            z = _dot(lr_d, gkw_ref[d]) + gkb_ref[d]
            g_scr[d, rows, :] = _log_sigmoid(z) / GLA_GATE_NORM
        return carry
    lax.fori_loop(0, n_blk, gates, 0)

    for d in range(2):
        _scan_direction(
            d,
            lambda rows: q_ref[0, rows, :].astype(F32) * scale,
            lambda rows: k_ref[0, rows, :].astype(F32),
            lambda rows: v_ref[0, rows, :],
            lambda rows, d=d: g_scr[d, rows, :],
            tri_ref, sel_ref, o_scr, n_ctx_blk=n_ctx_blk, n_blk=n_blk, dk=dk, dv=dv)
    _gated_norm_store(o_scr, gate_ref, gn_ref, out_ref, n_blk)


def _hgrn_kernel(q_ref, ff_ref, fb_ref, v_ref, gate_ref, lb_ref, gn_ref, tri_ref, sel_ref,
                 out_ref, o_scr, *, n_ctx_blk):
    t_rows, dk = q_ref.shape[1], q_ref.shape[2]
    dv = v_ref.shape[2]
    n_blk = t_rows // ROW_TILE
    f_refs = (ff_ref, fb_ref)

    def forget(d, rows):
        lb = lb_ref[d]
        return lb + (1.0 - lb) * jax.nn.sigmoid(f_refs[d][0, rows, :])

    for d in range(2):
        _scan_direction(
            d,
            lambda rows: q_ref[0, rows, :].astype(F32),
            lambda rows, d=d: 1.0 - forget(d, rows),
            lambda rows: v_ref[0, rows, :],
            lambda rows, d=d: jnp.log(forget(d, rows)),
            tri_ref, sel_ref, o_scr, n_ctx_blk=n_ctx_blk, n_blk=n_blk, dk=dk, dv=dv)
    _gated_norm_store(o_scr, gate_ref, gn_ref, out_ref, n_blk)


def _scan_consts():
    r = np.arange(ROW_TILE)
    same = (r[:, None] // LIN_CHUNK) == (r[None, :] // LIN_CHUNK)
    lower = same & (r[None, :] <= r[:, None])
    upper = same & (r[None, :] >= r[:, None])
    tri = np.stack([lower, upper, same]).astype(np.float32)
    sel = (r[:, None] // LIN_CHUNK == np.arange(LANES)[None, :]).astype(np.float32)
    return jnp.asarray(tri, BF16), jnp.asarray(sel, BF16)


def _even_mixers(p_bf, p_f32, gk_w, gk_b, gla_g, lb, hg_g, lc, cols, d_model):
    b, t, _ = p_bf.shape
    tri, sel = _scan_consts()
    n_ctx_blk = lc // ROW_TILE
    gla_dk = d_model // 2 // GLA_HEADS
    gla_dv = d_model // GLA_HEADS
    hg_heads = d_model // HG_DIM
    const2 = lambda i, h: (0, 0)
    const3 = lambda i, h: (0, 0, 0)

    def col(name, width):
        assert cols[name] % width == 0
        return cols[name] // width

    def head_spec(name, width):
        base = col(name, width)
        return pl.BlockSpec((1, t, width), lambda i, h: (i, 0, base + h))

    a_gla = pl.pallas_call(
        functools.partial(_gla_kernel, n_ctx_blk=n_ctx_blk),
        grid=(b, GLA_HEADS),
        in_specs=[head_spec("aq", gla_dk), head_spec("ak", gla_dk), head_spec("av", gla_dv),
                  head_spec("agate", gla_dv),
                  pl.BlockSpec((1, t, LANES), lambda i, h: (i, 0, col("alr", LANES))),
                  pl.BlockSpec((2, GLA_GATE_RANK, gla_dk), lambda i, h: (0, 0, h)),
                  pl.BlockSpec((2, 1, gla_dk), lambda i, h: (0, 0, h)),
                  pl.BlockSpec((1, gla_dv), const2),
                  pl.BlockSpec(tri.shape, const3),
                  pl.BlockSpec(sel.shape, const2)],
        out_specs=pl.BlockSpec((1, t, gla_dv), lambda i, h: (i, 0, h)),
        out_shape=jax.ShapeDtypeStruct((b, t, d_model), BF16),
        scratch_shapes=[pltpu.VMEM((2, t, gla_dk), F32), pltpu.VMEM((t, gla_dv), F32)],
        compiler_params=_cparams("arbitrary", "arbitrary"),
        name="gla_scan",
    )(p_bf, p_bf, p_bf, p_bf, p_f32, gk_w.astype(BF16), gk_b.reshape(2, 1, -1), gla_g.reshape(1, -1), tri, sel)

    a_hg = pl.pallas_call(
        functools.partial(_hgrn_kernel, n_ctx_blk=n_ctx_blk),
        grid=(b, hg_heads),
        in_specs=[head_spec("bq", HG_DIM), head_spec("bf0", HG_DIM), head_spec("bf1", HG_DIM),
                  head_spec("bi", HG_DIM), head_spec("bgate", HG_DIM),
                  pl.BlockSpec((2, 1, HG_DIM), lambda i, h: (0, 0, h)),
                  pl.BlockSpec((1, HG_DIM), const2),
                  pl.BlockSpec(tri.shape, const3),
                  pl.BlockSpec(sel.shape, const2)],
        out_specs=pl.BlockSpec((1, t, HG_DIM), lambda i, h: (i, 0, h)),
        out_shape=jax.ShapeDtypeStruct((b, t, d_model), BF16),
        scratch_shapes=[pltpu.VMEM((t, HG_DIM), F32)],
        compiler_params=_cparams("arbitrary", "arbitrary"),
        name="hgrn_scan",
    )(p_bf, p_f32, p_f32, p_bf, p_bf, lb.reshape(2, 1, -1), hg_g.reshape(1, -1), tri, sel)
    return a_gla, a_hg


def _even_out_kernel(a_ref, b_ref, wa_ref, wb_ref, x_ref, mod_ref, o_ref):
    y = _dot(a_ref[0], wa_ref[...]) + _dot(b_ref[0], wb_ref[...])
    o_ref[0] = x_ref[0] + mod_ref[0, 0, 2:3, :] * y


def _even_out(a_gla, a_hg, w_out, xs, mods, lc):
    b, t, d = xs.shape
    wa, wb = w_out[:a_gla.shape[2]].astype(BF16), w_out[a_gla.shape[2]:].astype(BF16)
    row = lambda i, r: (i, r, 0)
    const2 = lambda i, r: (0, 0)
    return pl.pallas_call(
        _even_out_kernel,
        grid=(b, t // ROW_TILE),
        in_specs=[pl.BlockSpec((1, ROW_TILE, a_gla.shape[2]), row),
                  pl.BlockSpec((1, ROW_TILE, a_hg.shape[2]), row),
                  pl.BlockSpec(wa.shape, const2),
                  pl.BlockSpec(wb.shape, const2),
                  pl.BlockSpec((1, ROW_TILE, d), row),
                  pl.BlockSpec((1, 1, 3, d), lambda i, r: (i, jnp.where(r * ROW_TILE >= lc, 1, 0), 0, 0))],
        out_specs=pl.BlockSpec((1, ROW_TILE, d), row),
        out_shape=jax.ShapeDtypeStruct((b, t, d), F32),
        compiler_params=_cparams("arbitrary", "arbitrary"),
        name="even_out",
    )(a_gla, a_hg, wa, wb, xs, mods)


def _rope(xf, cos, sin):
    w = xf.shape[1]
    reps = w // LANES
    cos_w = jnp.concatenate([cos] * reps, axis=1) if reps > 1 else cos
    sin_w = jnp.concatenate([sin] * reps, axis=1) if reps > 1 else sin
    lane = lax.broadcasted_iota(jnp.int32, xf.shape, 1)
    quarter = ATT_HEAD_DIM // 4
    partner = jnp.where(lane % (2 * quarter) < quarter,
                        pltpu.roll(xf, w - quarter, 1), pltpu.roll(xf, quarter, 1))
    return xf * cos_w + partner * sin_w


def _attn_kernel(q_ref, gate_ref, k_ref, v_ref, cq_ref, sq_ref, ck_ref, sk_ref, sink_ref, o_ref, kr_scr, *, lc):
    n = pl.program_id(1)
    length = k_ref.shape[1] - lc
    kv_heads = k_ref.shape[2] // ATT_HEAD_DIM
    band = 3 * ATT_BLOCK

    @pl.when(n == 0)
    def _():
        def body(i, carry):
            rows = pl.ds(pl.multiple_of(i * ROW_TILE, ROW_TILE), ROW_TILE)
            kf = k_ref[0, pl.ds(pl.multiple_of(lc + i * ROW_TILE, ROW_TILE), ROW_TILE), :].astype(F32)
            kr_scr[rows, :] = _rope(kf, ck_ref[rows, :], sk_ref[rows, :]).astype(BF16)
            return carry
        lax.fori_loop(0, length // ROW_TILE, body, 0)

    start = pl.multiple_of(jnp.clip((n - 1) * ATT_BLOCK, 0, length - band), ATT_BLOCK)
    q = (_rope(q_ref[0].astype(F32), cq_ref[...], sq_ref[...]) * ATT_HEAD_DIM ** -0.5).astype(BF16)
    qpos = n * ATT_BLOCK + lax.broadcasted_iota(jnp.int32, (ATT_BLOCK, band), 0)
    kpos = start + lax.broadcasted_iota(jnp.int32, (ATT_BLOCK, band), 1)
    valid = jnp.abs(qpos - kpos) <= WINDOW
    valid = jnp.concatenate([valid] * ATT_GROUP, axis=0)
    outs = []
    for hk in range(kv_heads):
        cs = slice(hk * ATT_HEAD_DIM, (hk + 1) * ATT_HEAD_DIM)
        k_c, v_c = k_ref[0, 0:lc, cs], v_ref[0, 0:lc, cs]
        k_b = kr_scr[pl.ds(start, band), cs]
        v_b = v_ref[0, pl.ds(pl.multiple_of(lc + start, ATT_BLOCK), band), cs]
        heads = [hk * ATT_GROUP + g for g in range(ATT_GROUP)]
        q4 = jnp.concatenate([q[:, h * ATT_HEAD_DIM:(h + 1) * ATT_HEAD_DIM] for h in heads], axis=0)
        sink = jnp.concatenate([jnp.broadcast_to(sink_ref[h:h + 1, 0:1], (ATT_BLOCK, 1)) for h in heads], axis=0)
        s_c = _dot_nt(q4, k_c)
        s_b = jnp.where(valid, _dot_nt(q4, k_b), NEG_INF)
        m = jnp.maximum(jnp.maximum(jnp.max(s_c, axis=-1, keepdims=True),
                                    jnp.max(s_b, axis=-1, keepdims=True)), sink)
        p_c, p_b = jnp.exp(s_c - m), jnp.exp(s_b - m)
        den = jnp.sum(p_c, axis=-1, keepdims=True) + jnp.sum(p_b, axis=-1, keepdims=True) + jnp.exp(sink - m)
        o4 = (_dot(p_c.astype(BF16), v_c) + _dot(p_b.astype(BF16), v_b)) / den
        outs.extend(o4[g * ATT_BLOCK:(g + 1) * ATT_BLOCK] for g in range(ATT_GROUP))
    o = jnp.concatenate(outs, axis=1)
    o_ref[0] = (o * _silu(gate_ref[0].astype(F32))).astype(o_ref.dtype)


def _rope_tables(length, grid_w):
    quarter = ATT_HEAD_DIM // 4
    freqs = ROPE_BASE ** (-jnp.arange(quarter, dtype=F32) / quarter)
    pos = jnp.arange(length)
    ang_r = (pos // grid_w).astype(F32)[:, None] * freqs
    ang_c = (pos % grid_w).astype(F32)[:, None] * freqs
    cos = jnp.concatenate([jnp.cos(ang_r)] * 2 + [jnp.cos(ang_c)] * 2, axis=1)
    sin = jnp.concatenate([-jnp.sin(ang_r), jnp.sin(ang_r), -jnp.sin(ang_c), jnp.sin(ang_c)], axis=1)
    reps = LANES // ATT_HEAD_DIM
    return jnp.tile(cos, (1, reps)), jnp.tile(sin, (1, reps))


def _attention(p1, sink, lc, cols, d_model, grid_w):
    b, t, _ = p1.shape
    length = t - lc
    kv_w = d_model // ATT_GROUP
    assert length % ROW_TILE == 0 and length >= 3 * ATT_BLOCK and lc % ATT_BLOCK == 0
    cos, sin = _rope_tables(length, grid_w)
    sink_b = jnp.broadcast_to(sink.astype(F32)[:, None], (sink.shape[0], LANES))
    q_blk = lc // ATT_BLOCK
    qrow = lambda i, n: (i, q_blk + n, cols["q"] // d_model)
    grow = lambda i, n: (i, q_blk + n, cols["g_att"] // d_model)
    const2 = lambda i, n: (0, 0)
    return pl.pallas_call(
        functools.partial(_attn_kernel, lc=lc),
        grid=(b, length // ATT_BLOCK),
        in_specs=[pl.BlockSpec((1, ATT_BLOCK, d_model), qrow),
                  pl.BlockSpec((1, ATT_BLOCK, d_model), grow),
                  pl.BlockSpec((1, t, kv_w), lambda i, n: (i, 0, cols["k"] // kv_w)),
                  pl.BlockSpec((1, t, kv_w), lambda i, n: (i, 0, cols["v"] // kv_w)),
                  pl.BlockSpec((ATT_BLOCK, LANES), lambda i, n: (n, 0)),
                  pl.BlockSpec((ATT_BLOCK, LANES), lambda i, n: (n, 0)),
                  pl.BlockSpec((length, LANES), const2),
                  pl.BlockSpec((length, LANES), const2),
                  pl.BlockSpec(sink_b.shape, const2)],
        out_specs=pl.BlockSpec((1, ATT_BLOCK, d_model), lambda i, n: (i, n, 0)),
        out_shape=jax.ShapeDtypeStruct((b, length, d_model), BF16),
        scratch_shapes=[pltpu.VMEM((length, kv_w), BF16)],
        compiler_params=_cparams("arbitrary", "arbitrary"),
        name="window_attn",
    )(p1, p1, p1, p1, cos, sin, cos, sin, sink_b)


def _cmul(ar, ai, br, bi):
    return ar * br - ai * bi, ar * bi + ai * br


def _s5_operators(lam_re, lam_im, log_dt, b_re, b_im, c_re, c_im, d_skip):
    f32 = F32
    lam_re, lam_im, b_re, b_im, c_re, c_im = (a.astype(f32) for a in (lam_re, lam_im, b_re, b_im, c_re, c_im))
    n_groups, n_state, n_ch = b_re.shape
    n_blocks = n_groups // S5_GPB
    dt = jnp.exp(log_dt.astype(f32))[:, :, None]
    steps = jnp.arange(S5_CHUNK + 1, dtype=f32)[:, None, None, None]
    mag = jnp.exp(lam_re * dt * steps)
    pw_re, pw_im = mag * jnp.cos(lam_im * dt * steps), mag * jnp.sin(lam_im * dt * steps)
    a_re, a_im = pw_re[1], pw_im[1]
    inv = 1.0 / (lam_re * lam_re + lam_im * lam_im)
    co_re, co_im = _cmul(a_re - 1.0, a_im, lam_re * inv, -lam_im * inv)
    bb_re, bb_im = _cmul(co_re[..., None], co_im[..., None], b_re[None], b_im[None])
    e_re, e_im = _cmul(pw_re[..., None], pw_im[..., None], bb_re[None], bb_im[None])
    lag = S5_CHUNK
    k_lag = (jnp.einsum("gop,tdgpi->tdgoi", c_re, e_re[:lag], precision=HIGHEST)
             - jnp.einsum("gop,tdgpi->tdgoi", c_im, e_im[:lag], precision=HIGHEST))
    diag = k_lag[0, 0] + k_lag[0, 1] + d_skip.astype(f32)[:, :, None] * jnp.eye(n_ch, dtype=f32)
    t_in = np.arange(lag)[:, None]
    t_out = np.arange(lag)[None, :]
    delta = t_out - t_in
    k_fwd = k_lag[np.clip(delta, 0, lag - 1), 0]
    k_bwd = k_lag[np.clip(-delta, 0, lag - 1), 1]
    sel = jnp.asarray(np.sign(delta))[:, :, None, None, None]
    k_full = jnp.where(sel > 0, k_fwd, jnp.where(sel < 0, k_bwd, diag[None, None]))
    eye = jnp.eye(S5_GPB, dtype=f32)
    k6 = k_full.reshape(lag, lag, n_blocks, S5_GPB, n_ch, n_ch)
    toep = jnp.einsum("abGgoi,gh->Gagibho", k6, eye).reshape(n_blocks, S5_FOLD, S5_FOLD)
    n_re = jnp.stack([e_re[:lag][::-1, 0], e_re[:lag, 1]])
    n_im = jnp.stack([e_im[:lag][::-1, 0], e_im[:lag, 1]])
    n7 = jnp.stack([n_re, n_im], axis=2).reshape(2, lag, 2, n_blocks, S5_GPB, n_state, n_ch)
    to_state = jnp.einsum("dtrGgpi,gh->Gtgidrhp", n7, eye).reshape(n_blocks, S5_FOLD, 4 * S5_GPB * n_state)
    r_re = jnp.stack([pw_re[1:, 0], pw_re[1:, 1][::-1]])
    r_im = jnp.stack([pw_im[1:, 0], pw_im[1:, 1][::-1]])
    ca_re, ca_im = _cmul(c_re[None, None], c_im[None, None],
                         r_re[:, :, :, None, :], r_im[:, :, :, None, :])
    m7 = jnp.stack([ca_re, -ca_im], axis=1).reshape(2, 2, lag, n_blocks, S5_GPB, n_ch, n_state)
    readout = jnp.einsum("drtGgop,gh->Gdrgptho", m7, eye).reshape(n_blocks, 4 * S5_GPB * n_state, S5_FOLD)
    dec = jnp.stack([pw_re[lag], pw_im[lag]], axis=1)
    dec = dec.reshape(2, 2, n_blocks, S5_GPB * n_state).transpose(2, 0, 1, 3)
    dec = dec.reshape(n_blocks, 2, 1, 2 * S5_GPB * n_state)
    return toep.astype(BF16), to_state.astype(BF16), readout.astype(BF16), dec


def _bmm_kernel(a_ref, w_ref, o_ref):
    o_ref[0] = _dot(a_ref[0], w_ref[0]).astype(o_ref.dtype)


def _s5_state_kernel(xl_ref, dec_ref, xp_ref, *, batch, n_ctx_chunks):
    d = pl.program_id(1)
    n_chunks = xl_ref.shape[1] // batch
    half = xl_ref.shape[2] // 2
    a_re, a_im = dec_ref[0, 0, :, :half], dec_ref[0, 0, :, half:]

    def step(i, state):
        s_re, s_im = state
        c_bwd = jnp.where(i < n_ctx_chunks, n_ctx_chunks - 1 - i, n_chunks - 1 - (i - n_ctx_chunks))
        c = jnp.where(d == 0, i, c_bwd)
        rows = pl.ds(pl.multiple_of(c * batch, batch), batch)
        xp_ref[0, rows, :] = jnp.concatenate([s_re, s_im], axis=1).astype(xp_ref.dtype)
        xl = xl_ref[0, rows, :]
        return (a_re * s_re - a_im * s_im + xl[:, :half], a_re * s_im + a_im * s_re + xl[:, half:])

    zero = jnp.zeros((batch, half), F32)
    lax.fori_loop(0, n_chunks, step, (zero, zero))


def _s5_out_kernel(u_ref, xp_ref, toep_ref, read_ref, o_ref):
    o_ref[0] = _dot(u_ref[0], toep_ref[0]) + _dot(xp_ref[0], read_ref[0])


def _s5(u, lc, ops):
    toep, to_state, readout, dec = ops
    b, t, w = u.shape
    n_blocks = w // LANES
    n_chunks, n_ctx_chunks = t // S5_CHUNK, lc // S5_CHUNK
    rows = n_chunks * b
    state_w = to_state.shape[2]
    assert b % 16 == 0 and rows % ROW_TILE == 0 and (n_ctx_chunks * b) % ROW_TILE == 0
    u2 = u.reshape(b, n_chunks, S5_CHUNK, n_blocks, LANES).transpose(3, 1, 0, 2, 4).reshape(n_blocks, rows, S5_FOLD)
    blk_row = lambda g, r: (g, r, 0)
    blk_w = lambda g, r: (g, 0, 0)
    x_loc = pl.pallas_call(
        _bmm_kernel,
        grid=(n_blocks, rows // ROW_TILE),
        in_specs=[pl.BlockSpec((1, ROW_TILE, S5_FOLD), blk_row),
                  pl.BlockSpec((1, S5_FOLD, state_w), blk_w)],
        out_specs=pl.BlockSpec((1, ROW_TILE, state_w), blk_row),
        out_shape=jax.ShapeDtypeStruct((n_blocks, rows, state_w), F32),
        compiler_params=_cparams("arbitrary", "arbitrary"),
        name="s5_local_state",
    )(u2, to_state)
    x_prev = pl.pallas_call(
        functools.partial(_s5_state_kernel, batch=b, n_ctx_chunks=n_ctx_chunks),
        grid=(n_blocks, 2),
        in_specs=[pl.BlockSpec((1, rows, state_w // 2), lambda g, d: (g, 0, d)),
                  pl.BlockSpec((1, 1, 1, state_w // 2), lambda g, d: (g, d, 0, 0))],
        out_specs=pl.BlockSpec((1, rows, state_w // 2), lambda g, d: (g, 0, d)),
        out_shape=jax.ShapeDtypeStruct((n_blocks, rows, state_w), BF16),
        compiler_params=_cparams("arbitrary", "arbitrary"),
        name="s5_state_scan",
    )(x_loc, dec)
    ctx_tiles = n_ctx_chunks * b // ROW_TILE
    lat_rows = rows - n_ctx_chunks * b
    lat_row = lambda g, r: (g, ctx_tiles + r, 0)
    y2 = pl.pallas_call(
        _s5_out_kernel,
        grid=(n_blocks, lat_rows // ROW_TILE),
        in_specs=[pl.BlockSpec((1, ROW_TILE, S5_FOLD), lat_row),
                  pl.BlockSpec((1, ROW_TILE, state_w), lat_row),
                  pl.BlockSpec((1, S5_FOLD, S5_FOLD), blk_w),
                  pl.BlockSpec((1, state_w, S5_FOLD), blk_w)],
        out_specs=pl.BlockSpec((1, ROW_TILE, S5_FOLD), blk_row),
        out_shape=jax.ShapeDtypeStruct((n_blocks, lat_rows, S5_FOLD), F32),
        compiler_params=_cparams("arbitrary", "arbitrary"),
        name="s5_chunk_out",
    )(u2, x_prev, toep, readout)
    y = y2.reshape(n_blocks, n_chunks - n_ctx_chunks, b, S5_CHUNK, LANES).transpose(2, 1, 3, 0, 4)
    return y.reshape(b, t - lc, w)


def _odd_out_kernel(att_ref, y_ref, gs_ref, glu_ref, wa_ref, wb_ref, x_ref, mod_ref, fg_ref, o_ref):
    width = y_ref.shape[2]
    z = jax.nn.gelu(y_ref[0]).astype(BF16)
    ab = _dot(z, glu_ref[...])
    s5 = ab[:, :width] * jax.nn.sigmoid(ab[:, width:]) * _silu(gs_ref[0].astype(F32))
    y = _dot(att_ref[0], wa_ref[...]) + _dot(s5.astype(BF16), wb_ref[...])
    xo = x_ref[0] + mod_ref[0, 0, 2:3, :] * y
    ms = jnp.mean(xo * xo, axis=-1, keepdims=True)
    o_ref[0] = xo * lax.rsqrt(ms + EPS) * fg_ref[...]


def _odd_out(att, y_s5, p1, glu_w, w_out, xs, mods, final_g, lc, cols):
    b, length, d = att.shape
    wa, wb = w_out[:d].astype(BF16), w_out[d:].astype(BF16)
    lat = lc // ROW_TILE
    row = lambda i, r: (i, r, 0)
    const2 = lambda i, r: (0, 0)
    return pl.pallas_call(
        _odd_out_kernel,
        grid=(b, length // ROW_TILE),
        in_specs=[pl.BlockSpec((1, ROW_TILE, d), row),
                  pl.BlockSpec((1, ROW_TILE, d), row),
                  pl.BlockSpec((1, ROW_TILE, d), lambda i, r: (i, lat + r, cols["g_s5"] // d)),
                  pl.BlockSpec(glu_w.shape, const2),
                  pl.BlockSpec(wa.shape, const2),
                  pl.BlockSpec(wb.shape, const2),
                  pl.BlockSpec((1, ROW_TILE, d), lambda i, r: (i, lat + r, 0)),
                  pl.BlockSpec((1, 1, 3, d), lambda i, r: (i, 1, 0, 0)),
                  pl.BlockSpec((1, d), const2)],
        out_specs=pl.BlockSpec((1, ROW_TILE, d), row),
        out_shape=jax.ShapeDtypeStruct((b, length, d), F32),
        compiler_params=_cparams("arbitrary", "arbitrary"),
        name="odd_out",
    )(att, y_s5, p1, glu_w.astype(BF16), wa, wb, xs, mods, final_g.reshape(1, d))


def _offsets(names_widths):
    cols, off = {}, 0
    for name, width in names_widths:
        cols[name] = off
        off += width
    return cols, off


def kernel(x, c, ctx, c_ctx, ada_w, ada_b, norm_g, final_norm_g, ev_w_in, ev_w_out, gla_gk_w, gla_gk_b,
           gla_norm_g, hgrn_lb_raw, hgrn_norm_g, od_w_in, od_w_out, attn_sink, s5_lambda_re, s5_lambda_im,
           s5_log_dt, s5_b_re, s5_b_im, s5_c_re, s5_c_im, s5_d, s5_glu_w):
    b, length, d = x.shape
    lc = ctx.shape[1]
    assert ada_w.shape[0] == 2 and ev_w_in.shape[0] == 1 and od_w_in.shape[0] == 1
    grid_w = 64
    xs = jnp.concatenate([ctx, x], axis=1)

    pad = (-(b + 1)) % 8
    cc = jnp.concatenate([c, c_ctx[None], jnp.zeros((pad, d), c.dtype)], axis=0)
    mods_all = _ada_mods(cc, ada_w, ada_b)

    def layer_mods(layer):
        lat = mods_all[layer, :b].reshape(b, 1, 3, d)
        cx = jnp.broadcast_to(mods_all[layer, b].reshape(1, 1, 3, d), (b, 1, 3, d))
        return jnp.concatenate([cx, lat], axis=1)

    mods0 = layer_mods(0)
    half = d // 2
    w0 = ev_w_in[0]
    src, _ = _offsets([("aq", half), ("ak", half), ("av", d), ("alr", 2 * GLA_GATE_RANK), ("agate", d),
                       ("bq", d), ("bf", 2 * d), ("bi", d), ("bgate", d)])
    bf_names = [("aq", half), ("ak", half), ("av", d), ("agate", d), ("bq", d), ("bi", d), ("bgate", d)]
    cols_bf, _ = _offsets(bf_names)
    w0_bf = jnp.concatenate([w0[:, src[n]:src[n] + wd] for n, wd in bf_names], axis=1).astype(BF16)
    lr_pad = ROW_TILE - 2 * GLA_GATE_RANK
    cols_f32, _ = _offsets([("bf0", d), ("bf1", d), ("alr", ROW_TILE)])
    w0_f32 = jnp.concatenate([w0[:, src["bf"]:src["bf"] + 2 * d],
                              w0[:, src["alr"]:src["alr"] + 2 * GLA_GATE_RANK],
                              jnp.zeros((d, lr_pad), w0.dtype)], axis=1).astype(BF16)
    p_bf = _inproj(xs, mods0, norm_g[0], w0_bf, BF16, lc, tn=512)
    p_f32 = _inproj(xs, mods0, norm_g[0], w0_f32, F32, lc, tn=768)
    lb_all = jnp.cumsum(jax.nn.softmax(hgrn_lb_raw.astype(F32), axis=1), axis=1)
    cols0 = dict(cols_bf, **cols_f32)
    a_gla, a_hg = _even_mixers(p_bf, p_f32, gla_gk_w[0], gla_gk_b[0], gla_norm_g[0], lb_all[:, 0],
                               hgrn_norm_g[0], lc, cols0, d)
    xs1 = _even_out(a_gla, a_hg, ev_w_out[0], xs, mods0, lc)

    mods1 = layer_mods(1)
    kv_w = d // ATT_GROUP
    w1 = od_w_in[0]
    src1, _ = _offsets([("q", d), ("k", kv_w), ("v", kv_w), ("g_att", d), ("u", d), ("g_s5", d)])
    names1 = [("q", d), ("g_att", d), ("u", d), ("g_s5", d), ("k", kv_w), ("v", kv_w)]
    cols1, _ = _offsets(names1)
    w1_bf = jnp.concatenate([w1[:, src1[n]:src1[n] + wd] for n, wd in names1], axis=1).astype(BF16)
    p1 = _inproj(xs1, mods1, norm_g[1], w1_bf, BF16, lc, tn=512)
    att = _attention(p1, attn_sink[0], lc, cols1, d, grid_w)
    ops = _s5_operators(s5_lambda_re[0], s5_lambda_im[0], s5_log_dt[0], s5_b_re[0], s5_b_im[0],
                        s5_c_re[0], s5_c_im[0], s5_d[0])
    u = p1[:, :, cols1["u"]:cols1["u"] + d]
    y_s5 = _s5(u, lc, ops)
    return _odd_out(att, y_s5, p1, s5_glu_w[0], od_w_out[0], xs1, mods1, final_norm_g, lc, cols1)
```

```python
import functools
import math

import jax
import jax.numpy as jnp
import numpy as np
from jax import lax
from jax.experimental import pallas as pl
from jax.experimental.pallas import tpu as pltpu

EPS = 1e-6
LANES = 128
ROW_TILE = 256
LIN_CHUNK = 32
CHUNKS_PER_BLOCK = ROW_TILE // LIN_CHUNK
VMEM_LIMIT = 56 * 1024 * 1024

GLA_HEADS = 4
GLA_HEADS_PER_STEP = 1
HG_HEADS_PER_STEP = 2
GLA_GATE_RANK = 16
GLA_GATE_NORM = 16.0
HG_DIM = 128
ATT_HEAD_DIM = 64
ATT_GROUP = 4
WINDOW = 128
ATT_BLOCK = 128
ROPE_BASE = 10000.0
NEG_INF = -1e30
S5_GROUP_CH = 16
S5_STATE = 64
S5_CHUNK = 16
S5_GPB = LANES // S5_GROUP_CH
S5_FOLD = S5_CHUNK * LANES

F32 = jnp.float32
BF16 = jnp.bfloat16
HIGHEST = lax.Precision.HIGHEST


def _cparams(*sem):
    return pltpu.CompilerParams(dimension_semantics=sem, vmem_limit_bytes=VMEM_LIMIT)


def _dot(a, b):
    return jnp.dot(a, b, preferred_element_type=F32)


def _dot_nt(a, b):
    return lax.dot_general(a, b, (((1,), (1,)), ((), ())), preferred_element_type=F32)


def _dot_tn(a, b):
    return lax.dot_general(a, b, (((0,), (0,)), ((), ())), preferred_element_type=F32)


def _silu(v):
    return v * jax.nn.sigmoid(v)


def _split_hi_lo(v):
    hi = v.astype(BF16)
    lo = (v - hi.astype(F32)).astype(BF16)
    return hi, lo


def _ada_kernel(c_ref, w_ref, b_ref, o_ref):
    o_ref[0] = _dot(_silu(c_ref[...]).astype(BF16), w_ref[0]) + b_ref[0]


def _ada_mods(cc, ada_w, ada_b, tn=512):
    depth, d, n = ada_w.shape
    rows = cc.shape[0]
    return pl.pallas_call(
        _ada_kernel,
        grid=(depth, n // tn),
        in_specs=[pl.BlockSpec((rows, d), lambda l, j: (0, 0)),
                  pl.BlockSpec((1, d, tn), lambda l, j: (l, 0, j)),
                  pl.BlockSpec((1, 1, tn), lambda l, j: (l, 0, j))],
        out_specs=pl.BlockSpec((1, rows, tn), lambda l, j: (l, 0, j)),
        out_shape=jax.ShapeDtypeStruct((depth, rows, n), F32),
        compiler_params=_cparams("arbitrary", "arbitrary"),
        name="ada_mods",
    )(cc, ada_w.astype(BF16), ada_b.reshape(depth, 1, n))


def _inproj_kernel(x_ref, mod_ref, g_ref, w_ref, o_ref, h_scr, *, lc):
    n_tiles = x_ref.shape[1] // ROW_TILE

    @pl.when(pl.program_id(1) == 0)
    def _():
        def body(i, carry):
            r0 = pl.multiple_of(i * ROW_TILE, ROW_TILE)
            which = (r0 >= lc).astype(jnp.int32)
            xf = x_ref[0, pl.ds(r0, ROW_TILE), :]
            ms = jnp.mean(xf * xf, axis=-1, keepdims=True)
            shift = mod_ref[0, which, 0:1, :]
            scale = mod_ref[0, which, 1:2, :]
            h = xf * lax.rsqrt(ms + EPS) * g_ref[...] * (1.0 + scale) + shift
            h_scr[pl.ds(r0, ROW_TILE), :] = h.astype(BF16)
            return carry
        lax.fori_loop(0, n_tiles, body, 0)

    for m in range(n_tiles):
        rows = slice(m * ROW_TILE, (m + 1) * ROW_TILE)
        o_ref[0, rows, :] = _dot(h_scr[rows, :], w_ref[...]).astype(o_ref.dtype)


def _inproj(xs, mods, g, w, out_dtype, lc, tn):
    b, t, d = xs.shape
    n = w.shape[1]
    assert n % tn == 0 and t % ROW_TILE == 0 and lc % ROW_TILE == 0
    return pl.pallas_call(
        functools.partial(_inproj_kernel, lc=lc),
        grid=(b, n // tn),
        in_specs=[pl.BlockSpec((1, t, d), lambda i, j: (i, 0, 0)),
                  pl.BlockSpec((1, 2, 3, d), lambda i, j: (i, 0, 0, 0)),
                  pl.BlockSpec((1, d), lambda i, j: (0, 0)),
                  pl.BlockSpec((d, tn), lambda i, j: (0, j))],
        out_specs=pl.BlockSpec((1, t, tn), lambda i, j: (i, 0, j)),
        out_shape=jax.ShapeDtypeStruct((b, t, n), out_dtype),
        scratch_shapes=[pltpu.VMEM((t, d), BF16)],
        compiler_params=_cparams("arbitrary", "arbitrary"),
        name="inproj",
    )(xs, mods, g.reshape(1, d), w)


def _scan_block(d, i, s, q_of, k_of, v_of, g_of, tri_ref, o_dst, kv_scr, *, n_ctx_blk, n_blk):
    tri = tri_ref[d]
    if d == 0:
        blk = i
    else:
        blk = jnp.where(i < n_ctx_blk, n_ctx_blk - 1 - i, n_blk - 1 - (i - n_ctx_blk))
    rows = pl.ds(pl.multiple_of(blk * ROW_TILE, ROW_TILE), ROW_TILE)
    g = g_of(rows)
    q = q_of(rows)
    k = k_of(rows)
    v = v_of(rows)
    dk = g.shape[1]
    g_hi, g_lo = _split_hi_lo(g)
    cs = _dot(tri, jnp.concatenate([g_hi, g_lo], axis=1))
    bcum = cs[:, :dk] + cs[:, dk:]
    last = LIN_CHUNK - 1 if d == 0 else 0
    tot3 = bcum.reshape(CHUNKS_PER_BLOCK, LIN_CHUNK, dk)[:, last:last + 1, :]
    total = jnp.broadcast_to(tot3, (CHUNKS_PER_BLOCK, LIN_CHUNK, dk)).reshape(ROW_TILE, dk)
    q_dec = (q * jnp.exp(bcum)).astype(BF16)
    k_inv = (k * jnp.exp(-bcum)).astype(BF16)
    k_tail = (k * jnp.exp(total - bcum)).astype(BF16)
    att = jnp.where(tri > 0, _dot_nt(q_dec, k_inv), 0.0).astype(BF16)
    o_blk = _dot(att, v)
    tot8 = jnp.concatenate([tot3.reshape(CHUNKS_PER_BLOCK, dk),
                            jnp.zeros((LANES - CHUNKS_PER_BLOCK, dk), F32)], axis=0)
    dec_t = jnp.exp(tot8.T)
    for j in range(CHUNKS_PER_BLOCK):
        cr = slice(j * LIN_CHUNK, (j + 1) * LIN_CHUNK)
        kv_scr[j] = _dot_tn(k_tail[cr], v[cr])
    pieces = [None] * CHUNKS_PER_BLOCK
    order = range(CHUNKS_PER_BLOCK) if d == 0 else range(CHUNKS_PER_BLOCK - 1, -1, -1)
    for j in order:
        cr = slice(j * LIN_CHUNK, (j + 1) * LIN_CHUNK)
        pieces[j] = _dot(q_dec[cr], s.astype(BF16))
        s = s * dec_t[:, j:j + 1] + kv_scr[j]
    o_dst(rows, o_blk + jnp.concatenate(pieces, axis=0))
    return s


def _run_chains(chains, n_blk):
    def body(i, states):
        return tuple(step(i, s) for (step, _), s in zip(chains, states))
    lax.fori_loop(0, n_blk, body, tuple(jnp.zeros(shape, F32) for _, shape in chains))


def _gated_norm_store(o_scr, gate_ref, gn_ref, out_ref, n_blk, dv):
    heads = o_scr.shape[2] // dv

    def body(i, carry):
        rows = pl.ds(pl.multiple_of(i * ROW_TILE, ROW_TILE), ROW_TILE)
        o = o_scr[0, rows, :] + o_scr[1, rows, :]
        normed = []
        for h in range(heads):
            oh = o[:, h * dv:(h + 1) * dv]
            ms = jnp.mean(oh * oh, axis=-1, keepdims=True)
            normed.append(oh * lax.rsqrt(ms + EPS) * gn_ref[...])
        o = jnp.concatenate(normed, axis=1) if heads > 1 else normed[0]
        gate = gate_ref[0, rows, :].astype(F32)
        out_ref[0, rows, :] = (o * _silu(gate)).astype(out_ref.dtype)
        return carry
    lax.fori_loop(0, n_blk, body, 0)


def _log_sigmoid(v):
    return jnp.minimum(v, 0.0) - jnp.log1p(jnp.exp(-jnp.abs(v)))


def _gla_kernel(q_ref, k_ref, v_ref, gate_ref, lr_ref, gkw_ref, gkb_ref, gn_ref, tri_ref,
                out_ref, g_scr, o_scr, kv_scr, *, n_ctx_blk, dk, dv):
    t_rows = q_ref.shape[1]
    heads = q_ref.shape[2] // dk
    n_blk = t_rows // ROW_TILE
    scale = dk ** -0.5

    def gates(i, carry):
        rows = pl.ds(pl.multiple_of(i * ROW_TILE, ROW_TILE), ROW_TILE)
        lr = lr_ref[0, rows, :]
        for d in range(2):
            lr_d = lr[:, d * GLA_GATE_RANK:(d + 1) * GLA_GATE_RANK].astype(BF16)
            z = _dot(lr_d, gkw_ref[d]) + gkb_ref[d]
            g_scr[d, rows, :] = _log_sigmoid(z) / GLA_GATE_NORM
        return carry
    lax.fori_loop(0, n_blk, gates, 0)

    chains = []
    for h in range(heads):
        kc = slice(h * dk, (h + 1) * dk)
        vc = slice(h * dv, (h + 1) * dv)
        for d in range(2):
            def dst(rows, val, d=d, vc=vc):
                o_scr[d, rows, vc] = val
            step = functools.partial(
                _scan_block, d,
                q_of=lambda rows, kc=kc: q_ref[0, rows, kc].astype(F32) * scale,
                k_of=lambda rows, kc=kc: k_ref[0, rows, kc].astype(F32),
                v_of=lambda rows, vc=vc: v_ref[0, rows, vc],
                g_of=lambda rows, d=d, kc=kc: g_scr[d, rows, kc],
                tri_ref=tri_ref, o_dst=dst, kv_scr=kv_scr.at[len(chains)], n_ctx_blk=n_ctx_blk, n_blk=n_blk)
            chains.append((step, (dk, dv)))
    _run_chains(chains, n_blk)
    _gated_norm_store(o_scr, gate_ref, gn_ref, out_ref, n_blk, dv)


def _hgrn_kernel(q_ref, ff_ref, fb_ref, v_ref, gate_ref, lb_ref, gn_ref, tri_ref,
                 out_ref, o_scr, kv_scr, *, n_ctx_blk, dk):
    t_rows = q_ref.shape[1]
    heads = q_ref.shape[2] // dk
    n_blk = t_rows // ROW_TILE
    f_refs = (ff_ref, fb_ref)

    def forget(d, rows, hc):
        lb = lb_ref[d][:, hc]
        return lb + (1.0 - lb) * jax.nn.sigmoid(f_refs[d][0, rows, hc])

    chains = []
    for h in range(heads):
        hc = slice(h * dk, (h + 1) * dk)
        for d in range(2):
            def dst(rows, val, d=d, hc=hc):
                o_scr[d, rows, hc] = val
            step = functools.partial(
                _scan_block, d,
                q_of=lambda rows, hc=hc: q_ref[0, rows, hc].astype(F32),
                k_of=lambda rows, d=d, hc=hc: 1.0 - forget(d, rows, hc),
                v_of=lambda rows, hc=hc: v_ref[0, rows, hc],
                g_of=lambda rows, d=d, hc=hc: jnp.log(forget(d, rows, hc)),
                tri_ref=tri_ref, o_dst=dst, kv_scr=kv_scr.at[len(chains)], n_ctx_blk=n_ctx_blk, n_blk=n_blk)
            chains.append((step, (dk, dk)))
    _run_chains(chains, n_blk)
    _gated_norm_store(o_scr, gate_ref, gn_ref, out_ref, n_blk, dk)


def _scan_consts():
    r = np.arange(ROW_TILE)
    same = (r[:, None] // LIN_CHUNK) == (r[None, :] // LIN_CHUNK)
    lower = same & (r[None, :] <= r[:, None])
    upper = same & (r[None, :] >= r[:, None])
    return jnp.asarray(np.stack([lower, upper]).astype(np.float32), BF16)


def _even_mixers(p_bf, p_f32, gk_w, gk_b, gla_g, lb, hg_g, lc, cols, d_model):
    b, t, _ = p_bf.shape
    tri = _scan_consts()
    n_ctx_blk = lc // ROW_TILE
    gla_dk = d_model // 2 // GLA_HEADS
    gla_dv = d_model // GLA_HEADS
    hg_heads = d_model // HG_DIM
    const2 = lambda i, h: (0, 0)
    const3 = lambda i, h: (0, 0, 0)

    def head_spec(name, width):
        assert cols[name] % width == 0
        base = cols[name] // width
        return pl.BlockSpec((1, t, width), lambda i, h: (i, 0, base + h))

    gw = GLA_HEADS_PER_STEP
    a_gla = pl.pallas_call(
        functools.partial(_gla_kernel, n_ctx_blk=n_ctx_blk, dk=gla_dk, dv=gla_dv),
        grid=(b, GLA_HEADS // gw),
        in_specs=[head_spec("aq", gw * gla_dk), head_spec("ak", gw * gla_dk), head_spec("av", gw * gla_dv),
                  head_spec("agate", gw * gla_dv),
                  pl.BlockSpec((1, t, LANES), lambda i, h: (i, 0, cols["alr"] // LANES)),
                  pl.BlockSpec((2, GLA_GATE_RANK, gw * gla_dk), lambda i, h: (0, 0, h)),
                  pl.BlockSpec((2, 1, gw * gla_dk), lambda i, h: (0, 0, h)),
                  pl.BlockSpec((1, gla_dv), const2),
                  pl.BlockSpec(tri.shape, const3)],
        out_specs=pl.BlockSpec((1, t, gw * gla_dv), lambda i, h: (i, 0, h)),
        out_shape=jax.ShapeDtypeStruct((b, t, d_model), BF16),
        scratch_shapes=[pltpu.VMEM((2, t, gw * gla_dk), F32), pltpu.VMEM((2, t, gw * gla_dv), F32),
                        pltpu.VMEM((2 * gw, CHUNKS_PER_BLOCK, gla_dk, gla_dv), F32)],
        compiler_params=_cparams("arbitrary", "arbitrary"),
        name="gla_scan",
    )(p_bf, p_bf, p_bf, p_bf, p_f32, gk_w.astype(BF16), gk_b.reshape(2, 1, -1), gla_g.reshape(1, -1), tri)

    hw = HG_HEADS_PER_STEP
    a_hg = pl.pallas_call(
        functools.partial(_hgrn_kernel, n_ctx_blk=n_ctx_blk, dk=HG_DIM),
        grid=(b, hg_heads // hw),
        in_specs=[head_spec("bq", hw * HG_DIM), head_spec("bf0", hw * HG_DIM), head_spec("bf1", hw * HG_DIM),
                  head_spec("bi", hw * HG_DIM), head_spec("bgate", hw * HG_DIM),
                  pl.BlockSpec((2, 1, hw * HG_DIM), lambda i, h: (0, 0, h)),
                  pl.BlockSpec((1, HG_DIM), const2),
                  pl.BlockSpec(tri.shape, const3)],
        out_specs=pl.BlockSpec((1, t, hw * HG_DIM), lambda i, h: (i, 0, h)),
        out_shape=jax.ShapeDtypeStruct((b, t, d_model), BF16),
        scratch_shapes=[pltpu.VMEM((2, t, hw * HG_DIM), F32),
                        pltpu.VMEM((2 * hw, CHUNKS_PER_BLOCK, HG_DIM, HG_DIM), F32)],
        compiler_params=_cparams("arbitrary", "arbitrary"),
        name="hgrn_scan",
    )(p_bf, p_f32, p_f32, p_bf, p_bf, lb.reshape(2, 1, -1), hg_g.reshape(1, -1), tri)
    return a_gla, a_hg


def _even_out_kernel(a_ref, b_ref, wa_ref, wb_ref, x_ref, mod_ref, o_ref):
    y = _dot(a_ref[0], wa_ref[...]) + _dot(b_ref[0], wb_ref[...])
    o_ref[0] = x_ref[0] + mod_ref[0, 0, 2:3, :] * y


def _even_out(a_gla, a_hg, w_out, xs, mods, lc):
    b, t, d = xs.shape
    wa, wb = w_out[:a_gla.shape[2]].astype(BF16), w_out[a_gla.shape[2]:].astype(BF16)
    row = lambda i, r: (i, r, 0)
    const2 = lambda i, r: (0, 0)
    return pl.pallas_call(
        _even_out_kernel,
        grid=(b, t // ROW_TILE),
        in_specs=[pl.BlockSpec((1, ROW_TILE, a_gla.shape[2]), row),
                  pl.BlockSpec((1, ROW_TILE, a_hg.shape[2]), row),
                  pl.BlockSpec(wa.shape, const2),
                  pl.BlockSpec(wb.shape, const2),
                  pl.BlockSpec((1, ROW_TILE, d), row),
                  pl.BlockSpec((1, 1, 3, d), lambda i, r: (i, jnp.where(r * ROW_TILE >= lc, 1, 0), 0, 0))],
        out_specs=pl.BlockSpec((1, ROW_TILE, d), row),
        out_shape=jax.ShapeDtypeStruct((b, t, d), F32),
        compiler_params=_cparams("arbitrary", "arbitrary"),
        name="even_out",
    )(a_gla, a_hg, wa, wb, xs, mods)


def _rope(xf, cos, sin):
    w = xf.shape[1]
    reps = w // LANES
    cos_w = jnp.concatenate([cos] * reps, axis=1) if reps > 1 else cos
    sin_w = jnp.concatenate([sin] * reps, axis=1) if reps > 1 else sin
    lane = lax.broadcasted_iota(jnp.int32, xf.shape, 1)
    quarter = ATT_HEAD_DIM // 4
    partner = jnp.where(lane % (2 * quarter) < quarter,
                        pltpu.roll(xf, w - quarter, 1), pltpu.roll(xf, quarter, 1))
    return xf * cos_w + partner * sin_w


def _attn_kernel(q_ref, gate_ref, k_ref, v_ref, cq_ref, sq_ref, ck_ref, sk_ref, sink_ref, o_ref, kr_scr, *, lc):
    n = pl.program_id(1)
    length = k_ref.shape[1] - lc
    kv_heads = k_ref.shape[2] // ATT_HEAD_DIM
    band = 3 * ATT_BLOCK

    @pl.when(n == 0)
    def _():
        def body(i, carry):
            rows = pl.ds(pl.multiple_of(i * ROW_TILE, ROW_TILE), ROW_TILE)
            kf = k_ref[0, pl.ds(pl.multiple_of(lc + i * ROW_TILE, ROW_TILE), ROW_TILE), :].astype(F32)
            kr_scr[rows, :] = _rope(kf, ck_ref[rows, :], sk_ref[rows, :]).astype(BF16)
            return carry
        lax.fori_loop(0, length // ROW_TILE, body, 0)

    start = pl.multiple_of(jnp.clip((n - 1) * ATT_BLOCK, 0, length - band), ATT_BLOCK)
    q = (_rope(q_ref[0].astype(F32), cq_ref[...], sq_ref[...]) * ATT_HEAD_DIM ** -0.5).astype(BF16)
    qpos = n * ATT_BLOCK + lax.broadcasted_iota(jnp.int32, (ATT_BLOCK, band), 0)
    kpos = start + lax.broadcasted_iota(jnp.int32, (ATT_BLOCK, band), 1)
    valid = jnp.abs(qpos - kpos) <= WINDOW
    valid = jnp.concatenate([jnp.ones((ATT_BLOCK, lc), jnp.bool_), valid], axis=1)
    valid = jnp.concatenate([valid] * ATT_GROUP, axis=0)
    ones = jnp.ones((lc + band, ATT_HEAD_DIM), BF16)
    outs = []
    for hk in range(kv_heads):
        cs = slice(hk * ATT_HEAD_DIM, (hk + 1) * ATT_HEAD_DIM)
        k_all = jnp.concatenate([k_ref[0, 0:lc, cs], kr_scr[pl.ds(start, band), cs]], axis=0)
        v_all = jnp.concatenate([v_ref[0, 0:lc, cs],
                                 v_ref[0, pl.ds(pl.multiple_of(lc + start, ATT_BLOCK), band), cs]], axis=0)
        heads = [hk * ATT_GROUP + g for g in range(ATT_GROUP)]
        q4 = jnp.concatenate([q[:, h * ATT_HEAD_DIM:(h + 1) * ATT_HEAD_DIM] for h in heads], axis=0)
        sink = jnp.concatenate([jnp.broadcast_to(sink_ref[h:h + 1, 0:1], (ATT_BLOCK, 1)) for h in heads], axis=0)
        s = jnp.where(valid, _dot_nt(q4, k_all), NEG_INF)
        m = jnp.maximum(jnp.max(s, axis=-1, keepdims=True), sink)
        p = jnp.exp(s - m).astype(BF16)
        pv = _dot(p, jnp.concatenate([v_all, ones], axis=1))
        o4 = pv[:, :ATT_HEAD_DIM] / (pv[:, ATT_HEAD_DIM:] + jnp.exp(sink - m))
        outs.extend(o4[g * ATT_BLOCK:(g + 1) * ATT_BLOCK] for g in range(ATT_GROUP))
    o = jnp.concatenate(outs, axis=1)
    o_ref[0] = (o * _silu(gate_ref[0].astype(F32))).astype(o_ref.dtype)


def _rope_tables(length, grid_w):
    quarter = ATT_HEAD_DIM // 4
    freqs = ROPE_BASE ** (-jnp.arange(quarter, dtype=F32) / quarter)
    pos = jnp.arange(length)
    ang_r = (pos // grid_w).astype(F32)[:, None] * freqs
    ang_c = (pos % grid_w).astype(F32)[:, None] * freqs
    cos = jnp.concatenate([jnp.cos(ang_r)] * 2 + [jnp.cos(ang_c)] * 2, axis=1)
    sin = jnp.concatenate([-jnp.sin(ang_r), jnp.sin(ang_r), -jnp.sin(ang_c), jnp.sin(ang_c)], axis=1)
    reps = LANES // ATT_HEAD_DIM
    return jnp.tile(cos, (1, reps)), jnp.tile(sin, (1, reps))


def _attention(p1, sink, lc, cols, d_model, grid_w):
    b, t, _ = p1.shape
    length = t - lc
    kv_w = d_model // ATT_GROUP
    assert length % ROW_TILE == 0 and length >= 3 * ATT_BLOCK and lc % ATT_BLOCK == 0
    cos, sin = _rope_tables(length, grid_w)
    sink_b = jnp.broadcast_to(sink.astype(F32)[:, None], (sink.shape[0], LANES))
    q_blk = lc // ATT_BLOCK
    qrow = lambda i, n: (i, q_blk + n, cols["q"] // d_model)
    grow = lambda i, n: (i, q_blk + n, cols["g_att"] // d_model)
    const2 = lambda i, n: (0, 0)
    return pl.pallas_call(
        functools.partial(_attn_kernel, lc=lc),
        grid=(b, length // ATT_BLOCK),
        in_specs=[pl.BlockSpec((1, ATT_BLOCK, d_model), qrow),
                  pl.BlockSpec((1, ATT_BLOCK, d_model), grow),
                  pl.BlockSpec((1, t, kv_w), lambda i, n: (i, 0, cols["k"] // kv_w)),
                  pl.BlockSpec((1, t, kv_w), lambda i, n: (i, 0, cols["v"] // kv_w)),
                  pl.BlockSpec((ATT_BLOCK, LANES), lambda i, n: (n, 0)),
                  pl.BlockSpec((ATT_BLOCK, LANES), lambda i, n: (n, 0)),
                  pl.BlockSpec((length, LANES), const2),
                  pl.BlockSpec((length, LANES), const2),
                  pl.BlockSpec(sink_b.shape, const2)],
        out_specs=pl.BlockSpec((1, ATT_BLOCK, d_model), lambda i, n: (i, n, 0)),
        out_shape=jax.ShapeDtypeStruct((b, length, d_model), BF16),
        scratch_shapes=[pltpu.VMEM((length, kv_w), BF16)],
        compiler_params=_cparams("arbitrary", "arbitrary"),
        name="window_attn",
    )(p1, p1, p1, p1, cos, sin, cos, sin, sink_b)


def _cmul(ar, ai, br, bi):
    return ar * br - ai * bi, ar * bi + ai * br


def _s5_operators(lam_re, lam_im, log_dt, b_re, b_im, c_re, c_im, d_skip):
    f32 = F32
    lam_re, lam_im, b_re, b_im, c_re, c_im = (a.astype(f32) for a in (lam_re, lam_im, b_re, b_im, c_re, c_im))
    n_groups, n_state, n_ch = b_re.shape
    n_blocks = n_groups // S5_GPB
    dt = jnp.exp(log_dt.astype(f32))[:, :, None]
    steps = jnp.arange(S5_CHUNK + 1, dtype=f32)[:, None, None, None]
    mag = jnp.exp(lam_re * dt * steps)
    pw_re, pw_im = mag * jnp.cos(lam_im * dt * steps), mag * jnp.sin(lam_im * dt * steps)
    a_re, a_im = pw_re[1], pw_im[1]
    inv = 1.0 / (lam_re * lam_re + lam_im * lam_im)
    co_re, co_im = _cmul(a_re - 1.0, a_im, lam_re * inv, -lam_im * inv)
    bb_re, bb_im = _cmul(co_re[..., None], co_im[..., None], b_re[None], b_im[None])
    e_re, e_im = _cmul(pw_re[..., None], pw_im[..., None], bb_re[None], bb_im[None])
    lag = S5_CHUNK
    k_lag = (jnp.einsum("gop,tdgpi->tdgoi", c_re, e_re[:lag], precision=HIGHEST)
             - jnp.einsum("gop,tdgpi->tdgoi", c_im, e_im[:lag], precision=HIGHEST))
    diag = k_lag[0, 0] + k_lag[0, 1] + d_skip.astype(f32)[:, :, None] * jnp.eye(n_ch, dtype=f32)
    t_in = np.arange(lag)[:, None]
    t_out = np.arange(lag)[None, :]
    delta = t_out - t_in
    k_fwd = k_lag[np.clip(delta, 0, lag - 1), 0]
    k_bwd = k_lag[np.clip(-delta, 0, lag - 1), 1]
    sel = jnp.asarray(np.sign(delta))[:, :, None, None, None]
    k_full = jnp.where(sel > 0, k_fwd, jnp.where(sel < 0, k_bwd, diag[None, None]))
    state_w = 4 * S5_GPB * n_state
    k6 = k_full.reshape(lag, lag, n_blocks, S5_GPB, n_ch, n_ch)
    toep_c = k6.transpose(2, 0, 3, 5, 1, 4).reshape(n_blocks, S5_FOLD, lag * n_ch)
    n_re = jnp.stack([e_re[:lag][::-1, 0], e_re[:lag, 1]])
    n_im = jnp.stack([e_im[:lag][::-1, 0], e_im[:lag, 1]])
    n7 = jnp.stack([n_re, n_im], axis=2).reshape(2, lag, 2, n_blocks, S5_GPB, n_state, n_ch)
    state_c = n7.transpose(3, 1, 6, 0, 2, 4, 5).reshape(n_blocks, lag * n_ch, state_w)
    r_re = jnp.stack([pw_re[1:, 0], pw_re[1:, 1][::-1]])
    r_im = jnp.stack([pw_im[1:, 0], pw_im[1:, 1][::-1]])
    ca_re, ca_im = _cmul(c_re[None, None], c_im[None, None],
                         r_re[:, :, :, None, :], r_im[:, :, :, None, :])
    m7 = jnp.stack([ca_re, -ca_im], axis=1).reshape(2, 2, lag, n_blocks, S5_GPB, n_ch, n_state)
    read_c = m7.transpose(3, 0, 1, 4, 6, 2, 5).reshape(n_blocks, state_w, lag * n_ch)
    dec = jnp.stack([pw_re[lag], pw_im[lag]], axis=1)
    dec = dec.reshape(2, 2, n_blocks, S5_GPB * n_state).transpose(2, 0, 1, 3)
    dec = dec.reshape(n_blocks, 2, 1, 2 * S5_GPB * n_state)
    c = np.arange(S5_FOLD)
    src = (c // LANES) * n_ch + c % n_ch
    spread = jnp.asarray((np.arange(lag * n_ch)[:, None] == src[None, :]).astype(np.float32), BF16)
    return toep_c.astype(BF16), state_c.astype(BF16), read_c.astype(BF16), dec, spread


def _group_of(shape, axis, span):
    return (lax.broadcasted_iota(jnp.int32, shape, axis) // span) % S5_GPB


def _s5_local_state_kernel(u_ref, sc_ref, spread_ref, o_ref, w_scr):
    @pl.when(pl.program_id(1) == 0)
    def _():
        n_state = sc_ref.shape[2] // (4 * S5_GPB)
        for m in range(S5_FOLD // ROW_TILE):
            rows = slice(m * ROW_TILE, (m + 1) * ROW_TILE)
            full = _dot_tn(spread_ref[:, rows], sc_ref[0])
            keep = _group_of(full.shape, 0, S5_GROUP_CH) == _group_of(full.shape, 1, n_state)
            w_scr[rows, :] = jnp.where(keep, full, 0.0).astype(BF16)
    o_ref[0] = _dot(u_ref[0], w_scr[...])


def _s5_state_kernel(xl_ref, dec_ref, xp_ref, *, batch, n_ctx_chunks):
    d = pl.program_id(1)
    n_chunks = xl_ref.shape[1] // batch
    half = xl_ref.shape[2] // 2
    a_re, a_im = dec_ref[0, 0, :, :half], dec_ref[0, 0, :, half:]

    def step(i, state):
        s_re, s_im = state
        c_bwd = jnp.where(i < n_ctx_chunks, n_ctx_chunks - 1 - i, n_chunks - 1 - (i - n_ctx_chunks))
        c = jnp.where(d == 0, i, c_bwd)
        rows = pl.ds(pl.multiple_of(c * batch, batch), batch)
        xp_ref[0, rows, :] = jnp.concatenate([s_re, s_im], axis=1).astype(xp_ref.dtype)
        xl = xl_ref[0, rows, :]
        return (a_re * s_re - a_im * s_im + xl[:, :half], a_re * s_im + a_im * s_re + xl[:, half:])

    zero = jnp.zeros((batch, half), F32)
    lax.fori_loop(0, n_chunks, step, (zero, zero))


def _s5_out_kernel(u_ref, xp_ref, tc_ref, rc_ref, spread_ref, o_ref, toep_scr, read_scr):
    @pl.when(pl.program_id(1) == 0)
    def _():
        n_state = rc_ref.shape[1] // (4 * S5_GPB)
        for m in range(S5_FOLD // ROW_TILE):
            cols = slice(m * ROW_TILE, (m + 1) * ROW_TILE)
            sp = spread_ref[:, cols]
            full = _dot(tc_ref[0], sp)
            col_g = _group_of(full.shape, 1, S5_GROUP_CH)
            toep_scr[:, cols] = jnp.where(_group_of(full.shape, 0, S5_GROUP_CH) == col_g, full, 0.0).astype(BF16)
            full = _dot(rc_ref[0], sp)
            read_scr[:, cols] = jnp.where(_group_of(full.shape, 0, n_state) == col_g, full, 0.0).astype(BF16)
    o_ref[0] = _dot(u_ref[0], toep_scr[...]) + _dot(xp_ref[0], read_scr[...])


def _s5(u, lc, ops):
    toep_c, state_c, read_c, dec, spread = ops
    b, t, w = u.shape
    n_blocks = w // LANES
    n_chunks, n_ctx_chunks = t // S5_CHUNK, lc // S5_CHUNK
    rows = n_chunks * b
    state_w = state_c.shape[2]
    assert b % 16 == 0 and rows % ROW_TILE == 0 and (n_ctx_chunks * b) % ROW_TILE == 0
    u2 = u.reshape(b, n_chunks, S5_CHUNK, n_blocks, LANES).transpose(3, 1, 0, 2, 4).reshape(n_blocks, rows, S5_FOLD)
    blk_row = lambda g, r: (g, r, 0)
    blk_w = lambda g, r: (g, 0, 0)
    const2 = lambda g, r: (0, 0)
    x_loc = pl.pallas_call(
        _s5_local_state_kernel,
        grid=(n_blocks, rows // ROW_TILE),
        in_specs=[pl.BlockSpec((1, ROW_TILE, S5_FOLD), blk_row),
                  pl.BlockSpec((1,) + state_c.shape[1:], blk_w),
                  pl.BlockSpec(spread.shape, const2)],
        out_specs=pl.BlockSpec((1, ROW_TILE, state_w), blk_row),
        out_shape=jax.ShapeDtypeStruct((n_blocks, rows, state_w), F32),
        scratch_shapes=[pltpu.VMEM((S5_FOLD, state_w), BF16)],
        compiler_params=_cparams("arbitrary", "arbitrary"),
        name="s5_local_state",
    )(u2, state_c, spread)
    x_prev = pl.pallas_call(
        functools.partial(_s5_state_kernel, batch=b, n_ctx_chunks=n_ctx_chunks),
        grid=(n_blocks, 2),
        in_specs=[pl.BlockSpec((1, rows, state_w // 2), lambda g, d: (g, 0, d)),
                  pl.BlockSpec((1, 1, 1, state_w // 2), lambda g, d: (g, d, 0, 0))],
        out_specs=pl.BlockSpec((1, rows, state_w // 2), lambda g, d: (g, 0, d)),
        out_shape=jax.ShapeDtypeStruct((n_blocks, rows, state_w), BF16),
        compiler_params=_cparams("arbitrary", "arbitrary"),
        name="s5_state_scan",
    )(x_loc, dec)
    ctx_tiles = n_ctx_chunks * b // ROW_TILE
    lat_rows = rows - n_ctx_chunks * b
    lat_row = lambda g, r: (g, ctx_tiles + r, 0)
    y2 = pl.pallas_call(
        _s5_out_kernel,
        grid=(n_blocks, lat_rows // ROW_TILE),
        in_specs=[pl.BlockSpec((1, ROW_TILE, S5_FOLD), lat_row),
                  pl.BlockSpec((1, ROW_TILE, state_w), lat_row),
                  pl.BlockSpec((1,) + toep_c.shape[1:], blk_w),
                  pl.BlockSpec((1,) + read_c.shape[1:], blk_w),
                  pl.BlockSpec(spread.shape, const2)],
        out_specs=pl.BlockSpec((1, ROW_TILE, S5_FOLD), blk_row),
        out_shape=jax.ShapeDtypeStruct((n_blocks, lat_rows, S5_FOLD), F32),
        scratch_shapes=[pltpu.VMEM((S5_FOLD, S5_FOLD), BF16), pltpu.VMEM((state_w, S5_FOLD), BF16)],
        compiler_params=_cparams("arbitrary", "arbitrary"),
        name="s5_chunk_out",
    )(u2, x_prev, toep_c, read_c, spread)
    y = y2.reshape(n_blocks, n_chunks - n_ctx_chunks, b, S5_CHUNK, LANES).transpose(2, 1, 3, 0, 4)
    return y.reshape(b, t - lc, w)


def _odd_out_kernel(att_ref, y_ref, gs_ref, glu_ref, wa_ref, wb_ref, x_ref, mod_ref, fg_ref, o_ref):
    width = y_ref.shape[2]
    z = jax.nn.gelu(y_ref[0]).astype(BF16)
    ab = _dot(z, glu_ref[...])
    s5 = ab[:, :width] * jax.nn.sigmoid(ab[:, width:]) * _silu(gs_ref[0].astype(F32))
    y = _dot(att_ref[0], wa_ref[...]) + _dot(s5.astype(BF16), wb_ref[...])
    xo = x_ref[0] + mod_ref[0, 0, 2:3, :] * y
    ms = jnp.mean(xo * xo, axis=-1, keepdims=True)
    o_ref[0] = xo * lax.rsqrt(ms + EPS) * fg_ref[...]


def _odd_out(att, y_s5, p1, glu_w, w_out, xs, mods, final_g, lc, cols):
    b, length, d = att.shape
    wa, wb = w_out[:d].astype(BF16), w_out[d:].astype(BF16)
    lat = lc // ROW_TILE
    row = lambda i, r: (i, r, 0)
    const2 = lambda i, r: (0, 0)
    return pl.pallas_call(
        _odd_out_kernel,
        grid=(b, length // ROW_TILE),
        in_specs=[pl.BlockSpec((1, ROW_TILE, d), row),
                  pl.BlockSpec((1, ROW_TILE, d), row),
                  pl.BlockSpec((1, ROW_TILE, d), lambda i, r: (i, lat + r, cols["g_s5"] // d)),
                  pl.BlockSpec(glu_w.shape, const2),
                  pl.BlockSpec(wa.shape, const2),
                  pl.BlockSpec(wb.shape, const2),
                  pl.BlockSpec((1, ROW_TILE, d), lambda i, r: (i, lat + r, 0)),
                  pl.BlockSpec((1, 1, 3, d), lambda i, r: (i, 1, 0, 0)),
                  pl.BlockSpec((1, d), const2)],
        out_specs=pl.BlockSpec((1, ROW_TILE, d), row),
        out_shape=jax.ShapeDtypeStruct((b, length, d), F32),
        compiler_params=_cparams("arbitrary", "arbitrary"),
        name="odd_out",
    )(att, y_s5, p1, glu_w.astype(BF16), wa, wb, xs, mods, final_g.reshape(1, d))


def _offsets(names_widths):
    cols, off = {}, 0
    for name, width in names_widths:
        cols[name] = off
        off += width
    return cols, off


def kernel(x, c, ctx, c_ctx, ada_w, ada_b, norm_g, final_norm_g, ev_w_in, ev_w_out, gla_gk_w, gla_gk_b,
           gla_norm_g, hgrn_lb_raw, hgrn_norm_g, od_w_in, od_w_out, attn_sink, s5_lambda_re, s5_lambda_im,
           s5_log_dt, s5_b_re, s5_b_im, s5_c_re, s5_c_im, s5_d, s5_glu_w):
    b, length, d = x.shape
    lc = ctx.shape[1]
    assert ada_w.shape[0] == 2 and ev_w_in.shape[0] == 1 and od_w_in.shape[0] == 1
    grid_w = 64
    xs = jnp.concatenate([ctx, x], axis=1)

    pad = (-(b + 1)) % 8
    cc = jnp.concatenate([c, c_ctx[None], jnp.zeros((pad, d), c.dtype)], axis=0)
    mods_all = _ada_mods(cc, ada_w, ada_b)

    def layer_mods(layer):
        lat = mods_all[layer, :b].reshape(b, 1, 3, d)
        cx = jnp.broadcast_to(mods_all[layer, b].reshape(1, 1, 3, d), (b, 1, 3, d))
        return jnp.concatenate([cx, lat], axis=1)

    mods0 = layer_mods(0)
    half = d // 2
    w0 = ev_w_in[0]
    src, _ = _offsets([("aq", half), ("ak", half), ("av", d), ("alr", 2 * GLA_GATE_RANK), ("agate", d),
                       ("bq", d), ("bf", 2 * d), ("bi", d), ("bgate", d)])
    bf_names = [("aq", half), ("ak", half), ("av", d), ("agate", d), ("bq", d), ("bi", d), ("bgate", d)]
    cols_bf, _ = _offsets(bf_names)
    w0_bf = jnp.concatenate([w0[:, src[n]:src[n] + wd] for n, wd in bf_names], axis=1).astype(BF16)
    lr_pad = ROW_TILE - 2 * GLA_GATE_RANK
    cols_f32, _ = _offsets([("bf0", d), ("bf1", d), ("alr", ROW_TILE)])
    w0_f32 = jnp.concatenate([w0[:, src["bf"]:src["bf"] + 2 * d],
                              w0[:, src["alr"]:src["alr"] + 2 * GLA_GATE_RANK],
                              jnp.zeros((d, lr_pad), w0.dtype)], axis=1).astype(BF16)
    p_bf = _inproj(xs, mods0, norm_g[0], w0_bf, BF16, lc, tn=512)
    p_f32 = _inproj(xs, mods0, norm_g[0], w0_f32, F32, lc, tn=768)
    lb_all = jnp.cumsum(jax.nn.softmax(hgrn_lb_raw.astype(F32), axis=1), axis=1)
    cols0 = dict(cols_bf, **cols_f32)
    a_gla, a_hg = _even_mixers(p_bf, p_f32, gla_gk_w[0], gla_gk_b[0], gla_norm_g[0], lb_all[:, 0],
                               hgrn_norm_g[0], lc, cols0, d)
    xs1 = _even_out(a_gla, a_hg, ev_w_out[0], xs, mods0, lc)

    mods1 = layer_mods(1)
    kv_w = d // ATT_GROUP
    w1 = od_w_in[0]
    src1, _ = _offsets([("q", d), ("k", kv_w), ("v", kv_w), ("g_att", d), ("u", d), ("g_s5", d)])
    names1 = [("q", d), ("g_att", d), ("u", d), ("g_s5", d), ("k", kv_w), ("v", kv_w)]
    cols1, _ = _offsets(names1)
    w1_bf = jnp.concatenate([w1[:, src1[n]:src1[n] + wd] for n, wd in names1], axis=1).astype(BF16)
    p1 = _inproj(xs1, mods1, norm_g[1], w1_bf, BF16, lc, tn=512)
    att = _attention(p1, attn_sink[0], lc, cols1, d, grid_w)
    ops = _s5_operators(s5_lambda_re[0], s5_lambda_im[0], s5_log_dt[0], s5_b_re[0], s5_b_im[0],
                        s5_c_re[0], s5_c_im[0], s5_d[0])
    u = p1[:, :, cols1["u"]:cols1["u"] + d]
    y_s5 = _s5(u, lc, ops)
    return _odd_out(att, y_s5, p1, s5_glu_w[0], od_w_out[0], xs1, mods1, final_norm_g, lc, cols1)
```

```python
import functools
import math

import jax
import jax.numpy as jnp
import numpy as np
from jax import lax
from jax.experimental import pallas as pl
from jax.experimental.pallas import tpu as pltpu

EPS = 1e-6
LANES = 128
ROW_TILE = 256
LIN_CHUNK = 32
CHUNKS_PER_BLOCK = ROW_TILE // LIN_CHUNK
VMEM_LIMIT = 56 * 1024 * 1024

GLA_HEADS = 4
GLA_HEADS_PER_STEP = 2
HG_HEADS_PER_STEP = 2
GLA_GATE_RANK = 16
GLA_GATE_NORM = 16.0
HG_DIM = 128
ATT_HEAD_DIM = 64
ATT_GROUP = 4
WINDOW = 128
ATT_BLOCK = 128
ROPE_BASE = 10000.0
NEG_INF = -1e30
LOG2E = 1.4426950408889634
S5_GROUP_CH = 16
S5_STATE = 64
S5_CHUNK = 16
S5_GPB = LANES // S5_GROUP_CH
S5_FOLD = S5_CHUNK * LANES

F32 = jnp.float32
BF16 = jnp.bfloat16
HIGHEST = lax.Precision.HIGHEST


def _cparams(*sem):
    return pltpu.CompilerParams(dimension_semantics=sem, vmem_limit_bytes=VMEM_LIMIT)


def _dot(a, b):
    return jnp.dot(a, b, preferred_element_type=F32)


def _dot_nt(a, b):
    return lax.dot_general(a, b, (((1,), (1,)), ((), ())), preferred_element_type=F32)


def _dot_tn(a, b):
    return lax.dot_general(a, b, (((0,), (0,)), ((), ())), preferred_element_type=F32)


def _silu(v):
    return v * jax.nn.sigmoid(v)


def _split_hi_lo(v):
    hi = v.astype(BF16)
    lo = (v - hi.astype(F32)).astype(BF16)
    return hi, lo


def _ada_kernel(c_ref, w_ref, b_ref, o_ref):
    o_ref[0] = _dot(_silu(c_ref[...]).astype(BF16), w_ref[0]) + b_ref[0]


def _ada_mods(cc, ada_w, ada_b, tn=512):
    depth, d, n = ada_w.shape
    rows = cc.shape[0]
    return pl.pallas_call(
        _ada_kernel,
        grid=(depth, n // tn),
        in_specs=[pl.BlockSpec((rows, d), lambda l, j: (0, 0)),
                  pl.BlockSpec((1, d, tn), lambda l, j: (l, 0, j)),
                  pl.BlockSpec((1, 1, tn), lambda l, j: (l, 0, j))],
        out_specs=pl.BlockSpec((1, rows, tn), lambda l, j: (l, 0, j)),
        out_shape=jax.ShapeDtypeStruct((depth, rows, n), F32),
        compiler_params=_cparams("arbitrary", "arbitrary"),
        name="ada_mods",
    )(cc, ada_w.astype(BF16), ada_b.reshape(depth, 1, n))


def _inproj_kernel(x_ref, mod_ref, g_ref, w_ref, o_ref, h_scr, *, lc):
    n_tiles = x_ref.shape[1] // ROW_TILE

    @pl.when(pl.program_id(1) == 0)
    def _():
        def body(i, carry):
            r0 = pl.multiple_of(i * ROW_TILE, ROW_TILE)
            which = (r0 >= lc).astype(jnp.int32)
            xf = x_ref[0, pl.ds(r0, ROW_TILE), :]
            ms = jnp.mean(xf * xf, axis=-1, keepdims=True)
            shift = mod_ref[0, which, 0:1, :]
            scale = mod_ref[0, which, 1:2, :]
            h = xf * lax.rsqrt(ms + EPS) * g_ref[...] * (1.0 + scale) + shift
            h_scr[pl.ds(r0, ROW_TILE), :] = h.astype(BF16)
            return carry
        lax.fori_loop(0, n_tiles, body, 0)

    for m in range(n_tiles):
        rows = slice(m * ROW_TILE, (m + 1) * ROW_TILE)
        o_ref[0, rows, :] = _dot(h_scr[rows, :], w_ref[...]).astype(o_ref.dtype)


def _inproj(xs, mods, g, w, out_dtype, lc, tn):
    b, t, d = xs.shape
    n = w.shape[1]
    assert n % tn == 0 and t % ROW_TILE == 0 and lc % ROW_TILE == 0
    return pl.pallas_call(
        functools.partial(_inproj_kernel, lc=lc),
        grid=(b, n // tn),
        in_specs=[pl.BlockSpec((1, t, d), lambda i, j: (i, 0, 0)),
                  pl.BlockSpec((1, 2, 3, d), lambda i, j: (i, 0, 0, 0)),
                  pl.BlockSpec((1, d), lambda i, j: (0, 0)),
                  pl.BlockSpec((d, tn), lambda i, j: (0, j))],
        out_specs=pl.BlockSpec((1, t, tn), lambda i, j: (i, 0, j)),
        out_shape=jax.ShapeDtypeStruct((b, t, n), out_dtype),
        scratch_shapes=[pltpu.VMEM((t, d), BF16)],
        compiler_params=_cparams("arbitrary", "arbitrary"),
        name="inproj",
    )(xs, mods, g.reshape(1, d), w)


def _scan_block(d, i, s, qkvg_of, tri_ref, o_dst, kv_scr, *, n_ctx_blk, n_blk):
    tri = tri_ref[d]
    if d == 0:
        blk = i
    else:
        blk = jnp.where(i < n_ctx_blk, n_ctx_blk - 1 - i, n_blk - 1 - (i - n_ctx_blk))
    rows = pl.ds(pl.multiple_of(blk * ROW_TILE, ROW_TILE), ROW_TILE)
    q, k, v, g = qkvg_of(rows)
    dk = g.shape[1]
    g_hi, g_lo = _split_hi_lo(g)
    cs = _dot(tri, jnp.concatenate([g_hi, g_lo], axis=1))
    yield
    bcum = cs[:, :dk] + cs[:, dk:]
    last = LIN_CHUNK - 1 if d == 0 else 0
    tot3 = bcum.reshape(CHUNKS_PER_BLOCK, LIN_CHUNK, dk)[:, last:last + 1, :]
    total = jnp.broadcast_to(tot3, (CHUNKS_PER_BLOCK, LIN_CHUNK, dk)).reshape(ROW_TILE, dk)
    q_dec = (q * jnp.exp(bcum)).astype(BF16)
    k_inv = (k * jnp.exp(-bcum)).astype(BF16)
    k_tail = (k * jnp.exp(total - bcum)).astype(BF16)
    scores = _dot_nt(q_dec, k_inv)
    yield
    for j in range(CHUNKS_PER_BLOCK):
        cr = slice(j * LIN_CHUNK, (j + 1) * LIN_CHUNK)
        kv_scr[j] = _dot_tn(k_tail[cr], v[cr])
    tot8 = jnp.concatenate([tot3.reshape(CHUNKS_PER_BLOCK, dk),
                            jnp.zeros((LANES - CHUNKS_PER_BLOCK, dk), F32)], axis=0)
    dec_t = jnp.exp(tot8.T)
    yield
    att = jnp.where(tri > 0, scores, 0.0).astype(BF16)
    o_blk = _dot(att, v)
    yield
    pieces = [None] * CHUNKS_PER_BLOCK
    order = range(CHUNKS_PER_BLOCK) if d == 0 else range(CHUNKS_PER_BLOCK - 1, -1, -1)
    for j in order:
        cr = slice(j * LIN_CHUNK, (j + 1) * LIN_CHUNK)
        pieces[j] = _dot(q_dec[cr], s.astype(BF16))
        s = s * dec_t[:, j:j + 1] + kv_scr[j]
        yield
    o_dst(rows, o_blk + jnp.concatenate(pieces, axis=0))
    return s


def _run_chains(chains, n_blk):
    def body(i, states):
        gens = [step(i, s) for (step, _), s in zip(chains, states)]
        out = [None] * len(gens)
        live = list(range(len(gens)))
        while live:
            for c in list(live):
                try:
                    next(gens[c])
                except StopIteration as done:
                    out[c] = done.value
                    live.remove(c)
        return tuple(out)
    lax.fori_loop(0, n_blk, body, tuple(jnp.zeros(shape, F32) for _, shape in chains))


def _gated_norm_store(o_scr, gate_ref, gn_ref, out_ref, n_blk, dv):
    heads = o_scr.shape[2] // dv

    def body(i, carry):
        rows = pl.ds(pl.multiple_of(i * ROW_TILE, ROW_TILE), ROW_TILE)
        o = o_scr[0, rows, :] + o_scr[1, rows, :]
        normed = []
        for h in range(heads):
            oh = o[:, h * dv:(h + 1) * dv]
            ms = jnp.mean(oh * oh, axis=-1, keepdims=True)
            normed.append(oh * lax.rsqrt(ms + EPS) * gn_ref[...])
        o = jnp.concatenate(normed, axis=1) if heads > 1 else normed[0]
        gate = gate_ref[0, rows, :].astype(F32)
        out_ref[0, rows, :] = (o * _silu(gate)).astype(out_ref.dtype)
        return carry
    lax.fori_loop(0, n_blk, body, 0)


def _log_sigmoid(v):
    return jnp.minimum(v, 0.0) - jnp.log1p(jnp.exp(-jnp.abs(v)))


def _gla_kernel(q_ref, k_ref, v_ref, gate_ref, lr_ref, gkw_ref, gkb_ref, gn_ref, tri_ref,
                out_ref, g_scr, o_scr, kv_scr, *, n_ctx_blk, dk, dv):
    t_rows = q_ref.shape[1]
    heads = q_ref.shape[2] // dk
    n_blk = t_rows // ROW_TILE
    scale = dk ** -0.5

    def gates(i, carry):
        rows = pl.ds(pl.multiple_of(i * ROW_TILE, ROW_TILE), ROW_TILE)
        lr = lr_ref[0, rows, :]
        for d in range(2):
            lr_d = lr[:, d * GLA_GATE_RANK:(d + 1) * GLA_GATE_RANK].astype(BF16)
            z = _dot(lr_d, gkw_ref[d]) + gkb_ref[d]
            g_scr[d, rows, :] = _log_sigmoid(z) / GLA_GATE_NORM
        return carry
    lax.fori_loop(0, n_blk, gates, 0)

    chains = []
    for h in range(heads):
        kc = slice(h * dk, (h + 1) * dk)
        vc = slice(h * dv, (h + 1) * dv)
        for d in range(2):
            def dst(rows, val, d=d, vc=vc):
                o_scr[d, rows, vc] = val
            def operands(rows, d=d, kc=kc, vc=vc):
                return (q_ref[0, rows, kc].astype(F32) * scale, k_ref[0, rows, kc].astype(F32),
                        v_ref[0, rows, vc], g_scr[d, rows, kc])
            step = functools.partial(
                _scan_block, d, qkvg_of=operands, tri_ref=tri_ref, o_dst=dst, kv_scr=kv_scr.at[len(chains)], n_ctx_blk=n_ctx_blk, n_blk=n_blk)
            chains.append((step, (dk, dv)))
    _run_chains(chains, n_blk)
    _gated_norm_store(o_scr, gate_ref, gn_ref, out_ref, n_blk, dv)


def _hgrn_kernel(q_ref, ff_ref, fb_ref, v_ref, gate_ref, lb_ref, gn_ref, tri_ref,
                 out_ref, o_scr, kv_scr, *, n_ctx_blk, dk):
    t_rows = q_ref.shape[1]
    heads = q_ref.shape[2] // dk
    n_blk = t_rows // ROW_TILE
    f_refs = (ff_ref, fb_ref)

    def forget(d, rows, hc):
        lb = lb_ref[d][:, hc]
        return lb + (1.0 - lb) * jax.nn.sigmoid(f_refs[d][0, rows, hc])

    chains = []
    for h in range(heads):
        hc = slice(h * dk, (h + 1) * dk)
        for d in range(2):
            def dst(rows, val, d=d, hc=hc):
                o_scr[d, rows, hc] = val
            def operands(rows, d=d, hc=hc):
                f = forget(d, rows, hc)
                return q_ref[0, rows, hc].astype(F32), 1.0 - f, v_ref[0, rows, hc], jnp.log(f)
            step = functools.partial(
                _scan_block, d, qkvg_of=operands, tri_ref=tri_ref, o_dst=dst, kv_scr=kv_scr.at[len(chains)], n_ctx_blk=n_ctx_blk, n_blk=n_blk)
            chains.append((step, (dk, dk)))
    _run_chains(chains, n_blk)
    _gated_norm_store(o_scr, gate_ref, gn_ref, out_ref, n_blk, dk)


def _scan_consts():
    r = np.arange(ROW_TILE)
    same = (r[:, None] // LIN_CHUNK) == (r[None, :] // LIN_CHUNK)
    lower = same & (r[None, :] <= r[:, None])
    upper = same & (r[None, :] >= r[:, None])
    return jnp.asarray(np.stack([lower, upper]).astype(np.float32), BF16)


def _even_mixers(p_bf, p_f32, gk_w, gk_b, gla_g, lb, hg_g, lc, cols, d_model):
    b, t, _ = p_bf.shape
    tri = _scan_consts()
    n_ctx_blk = lc // ROW_TILE
    gla_dk = d_model // 2 // GLA_HEADS
    gla_dv = d_model // GLA_HEADS
    hg_heads = d_model // HG_DIM
    const2 = lambda i, h: (0, 0)
    const3 = lambda i, h: (0, 0, 0)

    def head_spec(name, width):
        assert cols[name] % width == 0
        base = cols[name] // width
        return pl.BlockSpec((1, t, width), lambda i, h: (i, 0, base + h))

    gw = GLA_HEADS_PER_STEP
    a_gla = pl.pallas_call(
        functools.partial(_gla_kernel, n_ctx_blk=n_ctx_blk, dk=gla_dk, dv=gla_dv),
        grid=(b, GLA_HEADS // gw),
        in_specs=[head_spec("aq", gw * gla_dk), head_spec("ak", gw * gla_dk), head_spec("av", gw * gla_dv),
                  head_spec("agate", gw * gla_dv),
                  pl.BlockSpec((1, t, LANES), lambda i, h: (i, 0, cols["alr"] // LANES)),
                  pl.BlockSpec((2, GLA_GATE_RANK, gw * gla_dk), lambda i, h: (0, 0, h)),
                  pl.BlockSpec((2, 1, gw * gla_dk), lambda i, h: (0, 0, h)),
                  pl.BlockSpec((1, gla_dv), const2),
                  pl.BlockSpec(tri.shape, const3)],
        out_specs=pl.BlockSpec((1, t, gw * gla_dv), lambda i, h: (i, 0, h)),
        out_shape=jax.ShapeDtypeStruct((b, t, d_model), BF16),
        scratch_shapes=[pltpu.VMEM((2, t, gw * gla_dk), F32), pltpu.VMEM((2, t, gw * gla_dv), F32),
                        pltpu.VMEM((2 * gw, CHUNKS_PER_BLOCK, gla_dk, gla_dv), F32)],
        compiler_params=_cparams("arbitrary", "arbitrary"),
        name="gla_scan",
    )(p_bf, p_bf, p_bf, p_bf, p_f32, gk_w.astype(BF16), gk_b.reshape(2, 1, -1), gla_g.reshape(1, -1), tri)

    hw = HG_HEADS_PER_STEP
    a_hg = pl.pallas_call(
        functools.partial(_hgrn_kernel, n_ctx_blk=n_ctx_blk, dk=HG_DIM),
        grid=(b, hg_heads // hw),
        in_specs=[head_spec("bq", hw * HG_DIM), head_spec("bf0", hw * HG_DIM), head_spec("bf1", hw * HG_DIM),
                  head_spec("bi", hw * HG_DIM), head_spec("bgate", hw * HG_DIM),
                  pl.BlockSpec((2, 1, hw * HG_DIM), lambda i, h: (0, 0, h)),
                  pl.BlockSpec((1, HG_DIM), const2),
                  pl.BlockSpec(tri.shape, const3)],
        out_specs=pl.BlockSpec((1, t, hw * HG_DIM), lambda i, h: (i, 0, h)),
        out_shape=jax.ShapeDtypeStruct((b, t, d_model), BF16),
        scratch_shapes=[pltpu.VMEM((2, t, hw * HG_DIM), F32),
                        pltpu.VMEM((2 * hw, CHUNKS_PER_BLOCK, HG_DIM, HG_DIM), F32)],
        compiler_params=_cparams("arbitrary", "arbitrary"),
        name="hgrn_scan",
    )(p_bf, p_f32, p_f32, p_bf, p_bf, lb.reshape(2, 1, -1), hg_g.reshape(1, -1), tri)
    return a_gla, a_hg


def _even_out_kernel(a_ref, b_ref, wa_ref, wb_ref, x_ref, mod_ref, o_ref):
    y = _dot(a_ref[0], wa_ref[...]) + _dot(b_ref[0], wb_ref[...])
    o_ref[0] = x_ref[0] + mod_ref[0, 0, 2:3, :] * y


def _even_out(a_gla, a_hg, w_out, xs, mods, lc):
    b, t, d = xs.shape
    wa, wb = w_out[:a_gla.shape[2]].astype(BF16), w_out[a_gla.shape[2]:].astype(BF16)
    row = lambda i, r: (i, r, 0)
    const2 = lambda i, r: (0, 0)
    return pl.pallas_call(
        _even_out_kernel,
        grid=(b, t // ROW_TILE),
        in_specs=[pl.BlockSpec((1, ROW_TILE, a_gla.shape[2]), row),
                  pl.BlockSpec((1, ROW_TILE, a_hg.shape[2]), row),
                  pl.BlockSpec(wa.shape, const2),
                  pl.BlockSpec(wb.shape, const2),
                  pl.BlockSpec((1, ROW_TILE, d), row),
                  pl.BlockSpec((1, 1, 3, d), lambda i, r: (i, jnp.where(r * ROW_TILE >= lc, 1, 0), 0, 0))],
        out_specs=pl.BlockSpec((1, ROW_TILE, d), row),
        out_shape=jax.ShapeDtypeStruct((b, t, d), F32),
        compiler_params=_cparams("arbitrary", "arbitrary"),
        name="even_out",
    )(a_gla, a_hg, wa, wb, xs, mods)


def _rope(xf, cos, sin):
    w = xf.shape[1]
    reps = w // LANES
    cos_w = jnp.concatenate([cos] * reps, axis=1) if reps > 1 else cos
    sin_w = jnp.concatenate([sin] * reps, axis=1) if reps > 1 else sin
    lane = lax.broadcasted_iota(jnp.int32, xf.shape, 1)
    quarter = ATT_HEAD_DIM // 4
    partner = jnp.where(lane % (2 * quarter) < quarter,
                        pltpu.roll(xf, w - quarter, 1), pltpu.roll(xf, quarter, 1))
    return xf * cos_w + partner * sin_w


def _attn_kernel(q_ref, gate_ref, k_ref, v_ref, cq_ref, sq_ref, ck_ref, sk_ref, sink_ref, o_ref,
                 kr_scr, vt_scr, *, lc):
    n = pl.program_id(1)
    t_rows = k_ref.shape[1]
    length = t_rows - lc
    kv_heads = k_ref.shape[2] // ATT_HEAD_DIM
    band = 3 * ATT_BLOCK
    hd = ATT_HEAD_DIM

    @pl.when(n == 0)
    def _():
        def stage(i, carry):
            rows = pl.ds(pl.multiple_of(i * ROW_TILE, ROW_TILE), ROW_TILE)
            kf = k_ref[0, rows, :].astype(F32)
            lat = jnp.maximum(i * ROW_TILE - lc, 0)
            lrows = pl.ds(pl.multiple_of(lat, ROW_TILE), ROW_TILE)
            kr = jnp.where(i * ROW_TILE >= lc, _rope(kf, ck_ref[lrows, :], sk_ref[lrows, :]), kf).astype(BF16)
            vt = v_ref[0, rows, :].astype(F32).T.astype(BF16)
            for hk in range(kv_heads):
                kr_scr[hk, rows, :] = kr[:, hk * hd:(hk + 1) * hd]
                for half in range(ROW_TILE // ATT_BLOCK):
                    blk = i * (ROW_TILE // ATT_BLOCK) + half
                    vt_scr[hk, blk, 0:hd, :] = vt[hk * hd:(hk + 1) * hd, half * ATT_BLOCK:(half + 1) * ATT_BLOCK]
                    vt_scr[hk, blk, hd:2 * hd, :] = jnp.ones((hd, ATT_BLOCK), BF16)
            return carry
        lax.fori_loop(0, t_rows // ROW_TILE, stage, 0)

    start = pl.multiple_of(jnp.clip((n - 1) * ATT_BLOCK, 0, length - band), ATT_BLOCK)
    band_rows = pl.ds(pl.multiple_of(lc + start, ATT_BLOCK), band)
    band_blk = (lc + start) // ATT_BLOCK
    q = _rope(q_ref[0].astype(F32), cq_ref[...], sq_ref[...]) * (hd ** -0.5 * LOG2E)
    q_t = q.T.astype(BF16)
    kpos = start + lax.broadcasted_iota(jnp.int32, (band, ATT_BLOCK), 0)
    qpos = n * ATT_BLOCK + lax.broadcasted_iota(jnp.int32, (band, ATT_BLOCK), 1)
    valid = jnp.abs(qpos - kpos) <= WINDOW
    valid = jnp.concatenate([valid] * ATT_GROUP, axis=1)
    out_rows = []
    for hk in range(kv_heads):
        heads = [hk * ATT_GROUP + g for g in range(ATT_GROUP)]
        q4 = jnp.concatenate([q_t[h * hd:(h + 1) * hd, :] for h in heads], axis=1)
        sink = jnp.concatenate([sink_ref[h:h + 1, :] for h in heads], axis=1) * LOG2E
        s_c = _dot(kr_scr[hk, 0:lc, :], q4)
        s_b = jnp.where(valid, _dot(kr_scr[hk, band_rows, :], q4), NEG_INF)
        m = jnp.maximum(jnp.maximum(jnp.max(s_c, axis=0, keepdims=True),
                                    jnp.max(s_b, axis=0, keepdims=True)), sink)
        p = jnp.concatenate([jnp.exp2(s_c - m), jnp.exp2(s_b - m)], axis=0).astype(BF16)
        vt = jnp.concatenate([vt_scr[hk, c] for c in range(lc // ATT_BLOCK)]
                             + [vt_scr[hk, band_blk + c] for c in range(band // ATT_BLOCK)], axis=1)
        pv = _dot(vt, p)
        o_t = pv[0:hd] / (pv[hd:2 * hd] + jnp.exp2(sink - m))
        out_rows.extend(o_t[:, g * ATT_BLOCK:(g + 1) * ATT_BLOCK] for g in range(ATT_GROUP))
    o = jnp.concatenate(out_rows, axis=0).T
    o_ref[0] = (o * _silu(gate_ref[0].astype(F32))).astype(o_ref.dtype)


def _rope_tables(length, grid_w):
    quarter = ATT_HEAD_DIM // 4
    freqs = ROPE_BASE ** (-jnp.arange(quarter, dtype=F32) / quarter)
    pos = jnp.arange(length)
    ang_r = (pos // grid_w).astype(F32)[:, None] * freqs
    ang_c = (pos % grid_w).astype(F32)[:, None] * freqs
    cos = jnp.concatenate([jnp.cos(ang_r)] * 2 + [jnp.cos(ang_c)] * 2, axis=1)
    sin = jnp.concatenate([-jnp.sin(ang_r), jnp.sin(ang_r), -jnp.sin(ang_c), jnp.sin(ang_c)], axis=1)
    reps = LANES // ATT_HEAD_DIM
    return jnp.tile(cos, (1, reps)), jnp.tile(sin, (1, reps))


def _attention(p1, sink, lc, cols, d_model, grid_w):
    b, t, _ = p1.shape
    length = t - lc
    kv_w = d_model // ATT_GROUP
    assert length % ROW_TILE == 0 and length >= 3 * ATT_BLOCK and lc % ATT_BLOCK == 0
    cos, sin = _rope_tables(length, grid_w)
    sink_b = jnp.broadcast_to(sink.astype(F32)[:, None], (sink.shape[0], LANES))
    q_blk = lc // ATT_BLOCK
    qrow = lambda i, n: (i, q_blk + n, cols["q"] // d_model)
    grow = lambda i, n: (i, q_blk + n, cols["g_att"] // d_model)
    const2 = lambda i, n: (0, 0)
    return pl.pallas_call(
        functools.partial(_attn_kernel, lc=lc),
        grid=(b, length // ATT_BLOCK),
        in_specs=[pl.BlockSpec((1, ATT_BLOCK, d_model), qrow),
                  pl.BlockSpec((1, ATT_BLOCK, d_model), grow),
                  pl.BlockSpec((1, t, kv_w), lambda i, n: (i, 0, cols["k"] // kv_w)),
                  pl.BlockSpec((1, t, kv_w), lambda i, n: (i, 0, cols["v"] // kv_w)),
                  pl.BlockSpec((ATT_BLOCK, LANES), lambda i, n: (n, 0)),
                  pl.BlockSpec((ATT_BLOCK, LANES), lambda i, n: (n, 0)),
                  pl.BlockSpec((length, LANES), const2),
                  pl.BlockSpec((length, LANES), const2),
                  pl.BlockSpec(sink_b.shape, const2)],
        out_specs=pl.BlockSpec((1, ATT_BLOCK, d_model), lambda i, n: (i, n, 0)),
        out_shape=jax.ShapeDtypeStruct((b, length, d_model), BF16),
        scratch_shapes=[pltpu.VMEM((kv_w // ATT_HEAD_DIM, t, ATT_HEAD_DIM), BF16),
                        pltpu.VMEM((kv_w // ATT_HEAD_DIM, t // ATT_BLOCK, 2 * ATT_HEAD_DIM, ATT_BLOCK), BF16)],
        compiler_params=_cparams("arbitrary", "arbitrary"),
        name="window_attn",
    )(p1, p1, p1, p1, cos, sin, cos, sin, sink_b)


def _cmul(ar, ai, br, bi):
    return ar * br - ai * bi, ar * bi + ai * br


def _s5_operators(lam_re, lam_im, log_dt, b_re, b_im, c_re, c_im, d_skip):
    f32 = F32
    lam_re, lam_im, b_re, b_im, c_re, c_im = (a.astype(f32) for a in (lam_re, lam_im, b_re, b_im, c_re, c_im))
    n_groups, n_state, n_ch = b_re.shape
    n_blocks = n_groups // S5_GPB
    dt = jnp.exp(log_dt.astype(f32))[:, :, None]
    steps = jnp.arange(S5_CHUNK + 1, dtype=f32)[:, None, None, None]
    mag = jnp.exp(lam_re * dt * steps)
    pw_re, pw_im = mag * jnp.cos(lam_im * dt * steps), mag * jnp.sin(lam_im * dt * steps)
    a_re, a_im = pw_re[1], pw_im[1]
    inv = 1.0 / (lam_re * lam_re + lam_im * lam_im)
    co_re, co_im = _cmul(a_re - 1.0, a_im, lam_re * inv, -lam_im * inv)
    bb_re, bb_im = _cmul(co_re[..., None], co_im[..., None], b_re[None], b_im[None])
    e_re, e_im = _cmul(pw_re[..., None], pw_im[..., None], bb_re[None], bb_im[None])
    lag = S5_CHUNK
    k_lag = (jnp.einsum("gop,tdgpi->tdgoi", c_re, e_re[:lag], precision=HIGHEST)
             - jnp.einsum("gop,tdgpi->tdgoi", c_im, e_im[:lag], precision=HIGHEST))
    diag = k_lag[0, 0] + k_lag[0, 1] + d_skip.astype(f32)[:, :, None] * jnp.eye(n_ch, dtype=f32)
    by_lag = jnp.concatenate([k_lag[1:, 1][::-1], diag[None], k_lag[1:, 0]], axis=0)
    n_lags = 2 * lag - 1
    state_w = 4 * S5_GPB * n_state
    lag5 = by_lag.reshape(n_lags, n_blocks, S5_GPB, n_ch, n_ch)
    toep_c = lag5.transpose(1, 0, 2, 4, 3).reshape(n_blocks, n_lags * LANES, n_ch)
    n_re = jnp.stack([e_re[:lag][::-1, 0], e_re[:lag, 1]])
    n_im = jnp.stack([e_im[:lag][::-1, 0], e_im[:lag, 1]])
    n7 = jnp.stack([n_re, n_im], axis=2).reshape(2, lag, 2, n_blocks, S5_GPB, n_state, n_ch)
    state_c = n7.transpose(3, 1, 6, 0, 2, 4, 5).reshape(n_blocks, lag * n_ch, state_w)
    r_re = jnp.stack([pw_re[1:, 0], pw_re[1:, 1][::-1]])
    r_im = jnp.stack([pw_im[1:, 0], pw_im[1:, 1][::-1]])
    ca_re, ca_im = _cmul(c_re[None, None], c_im[None, None],
                         r_re[:, :, :, None, :], r_im[:, :, :, None, :])
    m7 = jnp.stack([ca_re, -ca_im], axis=1).reshape(2, 2, lag, n_blocks, S5_GPB, n_ch, n_state)
    read_c = m7.transpose(3, 0, 1, 4, 6, 2, 5).reshape(n_blocks, state_w, lag * n_ch)
    dec = jnp.stack([pw_re[lag], pw_im[lag]], axis=1)
    dec = dec.reshape(2, 2, n_blocks, S5_GPB * n_state).transpose(2, 0, 1, 3)
    dec = dec.reshape(n_blocks, 2, 1, 2 * S5_GPB * n_state)
    c = np.arange(S5_FOLD)
    src = (c // LANES) * n_ch + c % n_ch
    spread = jnp.asarray((np.arange(lag * n_ch)[:, None] == src[None, :]).astype(np.float32), BF16)
    return toep_c.astype(BF16), state_c.astype(BF16), read_c.astype(BF16), dec, spread


def _group_of(shape, axis, span):
    return (lax.broadcasted_iota(jnp.int32, shape, axis) // span) % S5_GPB


def _folded_rows(u_ref):
    return jnp.concatenate([u_ref[0, t] for t in range(S5_CHUNK)], axis=1)


def _s5_local_state_kernel(u_ref, sc_ref, spread_ref, o_ref, w_scr):
    @pl.when(pl.program_id(1) == 0)
    def _():
        n_state = sc_ref.shape[2] // (4 * S5_GPB)
        for m in range(S5_FOLD // ROW_TILE):
            rows = slice(m * ROW_TILE, (m + 1) * ROW_TILE)
            full = _dot_tn(spread_ref[:, rows], sc_ref[0])
            keep = _group_of(full.shape, 0, S5_GROUP_CH) == _group_of(full.shape, 1, n_state)
            w_scr[rows, :] = jnp.where(keep, full, 0.0).astype(BF16)
    o_ref[0] = _dot(_folded_rows(u_ref), w_scr[...])


def _s5_state_kernel(xl_ref, dec_ref, xp_ref, *, batch, n_ctx_chunks):
    d = pl.program_id(1)
    n_chunks = xl_ref.shape[1] // batch
    half = xl_ref.shape[2] // 2
    a_re, a_im = dec_ref[0, 0, :, :half], dec_ref[0, 0, :, half:]

    def step(i, state):
        s_re, s_im = state
        c_bwd = jnp.where(i < n_ctx_chunks, n_ctx_chunks - 1 - i, n_chunks - 1 - (i - n_ctx_chunks))
        c = jnp.where(d == 0, i, c_bwd)
        rows = pl.ds(pl.multiple_of(c * batch, batch), batch)
        xp_ref[0, rows, :] = jnp.concatenate([s_re, s_im], axis=1).astype(xp_ref.dtype)
        xl = xl_ref[0, rows, :]
        return (a_re * s_re - a_im * s_im + xl[:, :half], a_re * s_im + a_im * s_re + xl[:, half:])

    zero = jnp.zeros((batch, half), F32)
    lax.fori_loop(0, n_chunks, step, (zero, zero))


def _s5_out_kernel(u_ref, xp_ref, tc_ref, rc_ref, spread_ref, o_ref, lag_scr, toep_scr, read_scr):
    @pl.when(pl.program_id(1) == 0)
    def _():
        n_state = rc_ref.shape[1] // (4 * S5_GPB)
        n_ch = tc_ref.shape[2]
        full = _dot(tc_ref[0], spread_ref[0:n_ch, 0:LANES])
        keep = _group_of(full.shape, 0, S5_GROUP_CH) == _group_of(full.shape, 1, S5_GROUP_CH)
        lag_scr[...] = jnp.where(keep, full, 0.0).astype(BF16)
        for t_in in range(S5_CHUNK):
            for t_out in range(S5_CHUNK):
                lag = t_out - t_in + S5_CHUNK - 1
                toep_scr[t_in * LANES:(t_in + 1) * LANES, t_out * LANES:(t_out + 1) * LANES] = (
                    lag_scr[lag * LANES:(lag + 1) * LANES, :])
        for m in range(S5_FOLD // ROW_TILE):
            cols = slice(m * ROW_TILE, (m + 1) * ROW_TILE)
            full = _dot(rc_ref[0], spread_ref[:, cols])
            keep = _group_of(full.shape, 0, n_state) == _group_of(full.shape, 1, S5_GROUP_CH)
            read_scr[:, cols] = jnp.where(keep, full, 0.0).astype(BF16)
    y = _dot(_folded_rows(u_ref), toep_scr[...]) + _dot(xp_ref[0], read_scr[...])
    for t in range(S5_CHUNK):
        o_ref[0, t] = y[:, t * LANES:(t + 1) * LANES]


def _s5(u, lc, ops):
    toep_c, state_c, read_c, dec, spread = ops
    b, t, w = u.shape
    n_blocks = w // LANES
    n_chunks, n_ctx_chunks = t // S5_CHUNK, lc // S5_CHUNK
    rows = n_chunks * b
    state_w = state_c.shape[2]
    assert b % 16 == 0 and rows % ROW_TILE == 0 and (n_ctx_chunks * b) % ROW_TILE == 0
    u2 = u.reshape(b, n_chunks, S5_CHUNK, n_blocks, LANES).transpose(3, 2, 1, 0, 4)
    u2 = u2.reshape(n_blocks, S5_CHUNK, rows, LANES)
    blk_row = lambda g, r: (g, r, 0)
    blk_w = lambda g, r: (g, 0, 0)
    const2 = lambda g, r: (0, 0)
    x_loc = pl.pallas_call(
        _s5_local_state_kernel,
        grid=(n_blocks, rows // ROW_TILE),
        in_specs=[pl.BlockSpec((1, S5_CHUNK, ROW_TILE, LANES), lambda g, r: (g, 0, r, 0)),
                  pl.BlockSpec((1,) + state_c.shape[1:], blk_w),
                  pl.BlockSpec(spread.shape, const2)],
        out_specs=pl.BlockSpec((1, ROW_TILE, state_w), blk_row),
        out_shape=jax.ShapeDtypeStruct((n_blocks, rows, state_w), F32),
        scratch_shapes=[pltpu.VMEM((S5_FOLD, state_w), BF16)],
        compiler_params=_cparams("arbitrary", "arbitrary"),
        name="s5_local_state",
    )(u2, state_c, spread)
    x_prev = pl.pallas_call(
        functools.partial(_s5_state_kernel, batch=b, n_ctx_chunks=n_ctx_chunks),
        grid=(n_blocks, 2),
        in_specs=[pl.BlockSpec((1, rows, state_w // 2), lambda g, d: (g, 0, d)),
                  pl.BlockSpec((1, 1, 1, state_w // 2), lambda g, d: (g, d, 0, 0))],
        out_specs=pl.BlockSpec((1, rows, state_w // 2), lambda g, d: (g, 0, d)),
        out_shape=jax.ShapeDtypeStruct((n_blocks, rows, state_w), BF16),
        compiler_params=_cparams("arbitrary", "arbitrary"),
        name="s5_state_scan",
    )(x_loc, dec)
    ctx_tiles = n_ctx_chunks * b // ROW_TILE
    lat_rows = rows - n_ctx_chunks * b
    lat_row = lambda g, r: (g, ctx_tiles + r, 0)
    y2 = pl.pallas_call(
        _s5_out_kernel,
        grid=(n_blocks, lat_rows // ROW_TILE),
        in_specs=[pl.BlockSpec((1, S5_CHUNK, ROW_TILE, LANES), lambda g, r: (g, 0, ctx_tiles + r, 0)),
                  pl.BlockSpec((1, ROW_TILE, state_w), lat_row),
                  pl.BlockSpec((1,) + toep_c.shape[1:], blk_w),
                  pl.BlockSpec((1,) + read_c.shape[1:], blk_w),
                  pl.BlockSpec(spread.shape, const2)],
        out_specs=pl.BlockSpec((1, S5_CHUNK, ROW_TILE, LANES), lambda g, r: (g, 0, r, 0)),
        out_shape=jax.ShapeDtypeStruct((n_blocks, S5_CHUNK, lat_rows, LANES), F32),
        scratch_shapes=[pltpu.VMEM(toep_c.shape[1:2] + (LANES,), BF16),
                        pltpu.VMEM((S5_FOLD, S5_FOLD), BF16), pltpu.VMEM((state_w, S5_FOLD), BF16)],
        compiler_params=_cparams("arbitrary", "arbitrary"),
        name="s5_chunk_out",
    )(u2, x_prev, toep_c, read_c, spread)
    y = y2.reshape(n_blocks, S5_CHUNK, n_chunks - n_ctx_chunks, b, LANES).transpose(3, 2, 1, 0, 4)
    return y.reshape(b, t - lc, w)


def _odd_out_kernel(att_ref, y_ref, gs_ref, glu_ref, wa_ref, wb_ref, x_ref, mod_ref, fg_ref, o_ref):
    width = y_ref.shape[2]
    z = jax.nn.gelu(y_ref[0]).astype(BF16)
    ab = _dot(z, glu_ref[...])
    s5 = ab[:, :width] * jax.nn.sigmoid(ab[:, width:]) * _silu(gs_ref[0].astype(F32))
    y = _dot(att_ref[0], wa_ref[...]) + _dot(s5.astype(BF16), wb_ref[...])
    xo = x_ref[0] + mod_ref[0, 0, 2:3, :] * y
    ms = jnp.mean(xo * xo, axis=-1, keepdims=True)
    o_ref[0] = xo * lax.rsqrt(ms + EPS) * fg_ref[...]


def _odd_out(att, y_s5, p1, glu_w, w_out, xs, mods, final_g, lc, cols):
    b, length, d = att.shape
    wa, wb = w_out[:d].astype(BF16), w_out[d:].astype(BF16)
    lat = lc // ROW_TILE
    row = lambda i, r: (i, r, 0)
    const2 = lambda i, r: (0, 0)
    return pl.pallas_call(
        _odd_out_kernel,
        grid=(b, length // ROW_TILE),
        in_specs=[pl.BlockSpec((1, ROW_TILE, d), row),
                  pl.BlockSpec((1, ROW_TILE, d), row),
                  pl.BlockSpec((1, ROW_TILE, d), lambda i, r: (i, lat + r, cols["g_s5"] // d)),
                  pl.BlockSpec(glu_w.shape, const2),
                  pl.BlockSpec(wa.shape, const2),
                  pl.BlockSpec(wb.shape, const2),
                  pl.BlockSpec((1, ROW_TILE, d), lambda i, r: (i, lat + r, 0)),
                  pl.BlockSpec((1, 1, 3, d), lambda i, r: (i, 1, 0, 0)),
                  pl.BlockSpec((1, d), const2)],
        out_specs=pl.BlockSpec((1, ROW_TILE, d), row),
        out_shape=jax.ShapeDtypeStruct((b, length, d), F32),
        compiler_params=_cparams("arbitrary", "arbitrary"),
        name="odd_out",
    )(att, y_s5, p1, glu_w.astype(BF16), wa, wb, xs, mods, final_g.reshape(1, d))


def _offsets(names_widths):
    cols, off = {}, 0
    for name, width in names_widths:
        cols[name] = off
        off += width
    return cols, off


def kernel(x, c, ctx, c_ctx, ada_w, ada_b, norm_g, final_norm_g, ev_w_in, ev_w_out, gla_gk_w, gla_gk_b,
           gla_norm_g, hgrn_lb_raw, hgrn_norm_g, od_w_in, od_w_out, attn_sink, s5_lambda_re, s5_lambda_im,
           s5_log_dt, s5_b_re, s5_b_im, s5_c_re, s5_c_im, s5_d, s5_glu_w):
    b, length, d = x.shape
    lc = ctx.shape[1]
    assert ada_w.shape[0] == 2 and ev_w_in.shape[0] == 1 and od_w_in.shape[0] == 1
    grid_w = 64
    xs = jnp.concatenate([ctx, x], axis=1)

    pad = (-(b + 1)) % 8
    cc = jnp.concatenate([c, c_ctx[None], jnp.zeros((pad, d), c.dtype)], axis=0)
    mods_all = _ada_mods(cc, ada_w, ada_b)

    def layer_mods(layer):
        lat = mods_all[layer, :b].reshape(b, 1, 3, d)
        cx = jnp.broadcast_to(mods_all[layer, b].reshape(1, 1, 3, d), (b, 1, 3, d))
        return jnp.concatenate([cx, lat], axis=1)

    mods0 = layer_mods(0)
    half = d // 2
    w0 = ev_w_in[0]
    src, _ = _offsets([("aq", half), ("ak", half), ("av", d), ("alr", 2 * GLA_GATE_RANK), ("agate", d),
                       ("bq", d), ("bf", 2 * d), ("bi", d), ("bgate", d)])
    bf_names = [("aq", half), ("ak", half), ("av", d), ("agate", d), ("bq", d), ("bi", d), ("bgate", d)]
    cols_bf, _ = _offsets(bf_names)
    w0_bf = jnp.concatenate([w0[:, src[n]:src[n] + wd] for n, wd in bf_names], axis=1).astype(BF16)
    lr_pad = ROW_TILE - 2 * GLA_GATE_RANK
    cols_f32, _ = _offsets([("bf0", d), ("bf1", d), ("alr", ROW_TILE)])
    w0_f32 = jnp.concatenate([w0[:, src["bf"]:src["bf"] + 2 * d],
                              w0[:, src["alr"]:src["alr"] + 2 * GLA_GATE_RANK],
                              jnp.zeros((d, lr_pad), w0.dtype)], axis=1).astype(BF16)
    p_bf = _inproj(xs, mods0, norm_g[0], w0_bf, BF16, lc, tn=512)
    p_f32 = _inproj(xs, mods0, norm_g[0], w0_f32, F32, lc, tn=768)
    lb_all = jnp.cumsum(jax.nn.softmax(hgrn_lb_raw.astype(F32), axis=1), axis=1)
    cols0 = dict(cols_bf, **cols_f32)
    a_gla, a_hg = _even_mixers(p_bf, p_f32, gla_gk_w[0], gla_gk_b[0], gla_norm_g[0], lb_all[:, 0],
                               hgrn_norm_g[0], lc, cols0, d)
    xs1 = _even_out(a_gla, a_hg, ev_w_out[0], xs, mods0, lc)

    mods1 = layer_mods(1)
    kv_w = d // ATT_GROUP
    w1 = od_w_in[0]
    src1, _ = _offsets([("q", d), ("k", kv_w), ("v", kv_w), ("g_att", d), ("u", d), ("g_s5", d)])
    names1 = [("q", d), ("g_att", d), ("u", d), ("g_s5", d), ("k", kv_w), ("v", kv_w)]
    cols1, _ = _offsets(names1)
    w1_bf = jnp.concatenate([w1[:, src1[n]:src1[n] + wd] for n, wd in names1], axis=1).astype(BF16)
    p1 = _inproj(xs1, mods1, norm_g[1], w1_bf, BF16, lc, tn=512)
    att = _attention(p1, attn_sink[0], lc, cols1, d, grid_w)
    ops = _s5_operators(s5_lambda_re[0], s5_lambda_im[0], s5_log_dt[0], s5_b_re[0], s5_b_im[0],
                        s5_c_re[0], s5_c_im[0], s5_d[0])
    u = p1[:, :, cols1["u"]:cols1["u"] + d]
    y_s5 = _s5(u, lc, ops)
    return _odd_out(att, y_s5, p1, s5_glu_w[0], od_w_out[0], xs1, mods1, final_norm_g, lc, cols1)
```

```python
import functools
import math

import jax
import jax.numpy as jnp
import numpy as np
from jax import lax
from jax.experimental import pallas as pl
from jax.experimental.pallas import tpu as pltpu

EPS = 1e-6
LANES = 128
ROW_TILE = 256
PROJ_COLS = 512
LIN_CHUNK = 32
CHUNKS_PER_BLOCK = ROW_TILE // LIN_CHUNK
VMEM_LIMIT = 56 * 1024 * 1024

GLA_HEADS = 4
GLA_HEADS_PER_STEP = 2
HG_HEADS_PER_STEP = 2
GLA_GATE_RANK = 16
GLA_GATE_NORM = 16.0
HG_DIM = 128
ATT_HEAD_DIM = 64
ATT_GROUP = 4
WINDOW = 128
ATT_BLOCK = 128
ROPE_BASE = 10000.0
NEG_INF = -1e30
LOG2E = 1.4426950408889634
S5_GROUP_CH = 16
S5_STATE = 64
S5_CHUNK = 16
S5_GPB = LANES // S5_GROUP_CH
S5_FOLD = S5_CHUNK * LANES

F32 = jnp.float32
BF16 = jnp.bfloat16
HIGHEST = lax.Precision.HIGHEST


def _cparams(*sem):
    return pltpu.CompilerParams(dimension_semantics=sem, vmem_limit_bytes=VMEM_LIMIT)


def _dot(a, b):
    return jnp.dot(a, b, preferred_element_type=F32)


def _dot_nt(a, b):
    return lax.dot_general(a, b, (((1,), (1,)), ((), ())), preferred_element_type=F32)


def _dot_tn(a, b):
    return lax.dot_general(a, b, (((0,), (0,)), ((), ())), preferred_element_type=F32)


def _silu(v):
    return v * jax.nn.sigmoid(v)


def _split_hi_lo(v):
    hi = v.astype(BF16)
    lo = (v - hi.astype(F32)).astype(BF16)
    return hi, lo


def _ada_kernel(c_ref, w_ref, b_ref, o_ref):
    o_ref[0] = _dot(_silu(c_ref[...]).astype(BF16), w_ref[0]) + b_ref[0]


def _ada_mods(cc, ada_w, ada_b, tn=512):
    depth, d, n = ada_w.shape
    rows = cc.shape[0]
    return pl.pallas_call(
        _ada_kernel,
        grid=(depth, n // tn),
        in_specs=[pl.BlockSpec((rows, d), lambda l, j: (0, 0)),
                  pl.BlockSpec((1, d, tn), lambda l, j: (l, 0, j)),
                  pl.BlockSpec((1, 1, tn), lambda l, j: (l, 0, j))],
        out_specs=pl.BlockSpec((1, rows, tn), lambda l, j: (l, 0, j)),
        out_shape=jax.ShapeDtypeStruct((depth, rows, n), F32),
        compiler_params=_cparams("arbitrary", "arbitrary"),
        name="ada_mods",
    )(cc, ada_w.astype(BF16), ada_b.reshape(depth, 1, n))


def _stream_specs(stream, lc):
    if not isinstance(stream, tuple):
        return [pl.BlockSpec((1, ROW_TILE, stream.shape[2]), lambda i, r: (i, r, 0))]
    n_ctx = lc // ROW_TILE
    d = stream[0].shape[2]
    return [pl.BlockSpec((1, ROW_TILE, d), lambda i, r: (i, jnp.minimum(r, n_ctx - 1), 0)),
            pl.BlockSpec((1, ROW_TILE, d), lambda i, r: (i, jnp.maximum(r - n_ctx, 0), 0))]


def _stream_tile(x_refs, n_ctx_tiles):
    if len(x_refs) == 1:
        return x_refs[0][0]
    return jnp.where(pl.program_id(1) < n_ctx_tiles, x_refs[0][0], x_refs[1][0])


def _inproj_kernel(*refs, n_stream, n_ctx_tiles):
    x_refs, (mod_ref, g_ref), refs = refs[:n_stream], refs[n_stream:n_stream + 2], refs[n_stream + 2:]
    n_out = len(refs) // 2
    w_refs, o_refs = refs[:n_out], refs[n_out:]
    xf = _stream_tile(x_refs, n_ctx_tiles)
    ms = jnp.mean(xf * xf, axis=-1, keepdims=True)
    h = (xf * lax.rsqrt(ms + EPS) * g_ref[...] * (1.0 + mod_ref[0, 0, 1:2, :]) + mod_ref[0, 0, 0:1, :]).astype(BF16)
    for w_ref, o_ref in zip(w_refs, o_refs):
        width = w_ref.shape[1]
        step = math.gcd(width, PROJ_COLS)
        for c in range(0, width, step):
            o_ref[0, :, c:c + step] = _dot(h, w_ref[:, c:c + step]).astype(o_ref.dtype)


def _stream_dims(stream):
    parts = stream if isinstance(stream, tuple) else (stream,)
    return parts, parts[0].shape[0], sum(p.shape[1] for p in parts), parts[0].shape[2]


def _inproj(stream, mods, g, weights, out_dtypes, lc):
    parts, b, t, d = _stream_dims(stream)
    assert t % ROW_TILE == 0 and lc % ROW_TILE == 0
    row = lambda i, r: (i, r, 0)
    const2 = lambda i, r: (0, 0)
    return pl.pallas_call(
        functools.partial(_inproj_kernel, n_stream=len(parts), n_ctx_tiles=lc // ROW_TILE),
        grid=(b, t // ROW_TILE),
        in_specs=_stream_specs(stream, lc)
                 + [pl.BlockSpec((1, 1, 3, d), lambda i, r: (i, jnp.where(r * ROW_TILE >= lc, 1, 0), 0, 0)),
                    pl.BlockSpec((1, d), const2)]
                 + [pl.BlockSpec(w.shape, const2) for w in weights],
        out_specs=[pl.BlockSpec((1, ROW_TILE, w.shape[1]), row) for w in weights],
        out_shape=[jax.ShapeDtypeStruct((b, t, w.shape[1]), dt) for w, dt in zip(weights, out_dtypes)],
        compiler_params=_cparams("arbitrary", "arbitrary"),
        name="inproj",
    )(*parts, mods, g.reshape(1, d), *weights)


def _scan_block(d, i, s, qkvg_of, tri_ref, o_dst, kv_scr, *, n_ctx_blk, n_blk):
    tri = tri_ref[d]
    if d == 0:
        blk = i
    else:
        blk = jnp.where(i < n_ctx_blk, n_ctx_blk - 1 - i, n_blk - 1 - (i - n_ctx_blk))
    rows = pl.ds(pl.multiple_of(blk * ROW_TILE, ROW_TILE), ROW_TILE)
    q, k, v, g = qkvg_of(rows)
    dk = g.shape[1]
    g_hi, g_lo = _split_hi_lo(g)
    cs = _dot(tri, jnp.concatenate([g_hi, g_lo], axis=1))
    yield
    bcum = cs[:, :dk] + cs[:, dk:]
    last = LIN_CHUNK - 1 if d == 0 else 0
    tot3 = bcum.reshape(CHUNKS_PER_BLOCK, LIN_CHUNK, dk)[:, last:last + 1, :]
    total = jnp.broadcast_to(tot3, (CHUNKS_PER_BLOCK, LIN_CHUNK, dk)).reshape(ROW_TILE, dk)
    q_dec = (q * jnp.exp(bcum)).astype(BF16)
    k_inv = (k * jnp.exp(-bcum)).astype(BF16)
    k_tail = (k * jnp.exp(total - bcum)).astype(BF16)
    scores = _dot_nt(q_dec, k_inv)
    yield
    for j in range(CHUNKS_PER_BLOCK):
        cr = slice(j * LIN_CHUNK, (j + 1) * LIN_CHUNK)
        kv_scr[j] = _dot_tn(k_tail[cr], v[cr])
    tot8 = jnp.concatenate([tot3.reshape(CHUNKS_PER_BLOCK, dk),
                            jnp.zeros((LANES - CHUNKS_PER_BLOCK, dk), F32)], axis=0)
    dec_t = jnp.exp(tot8.T)
    yield
    att = jnp.where(tri > 0, scores, 0.0).astype(BF16)
    o_blk = _dot(att, v)
    yield
    pieces = [None] * CHUNKS_PER_BLOCK
    order = range(CHUNKS_PER_BLOCK) if d == 0 else range(CHUNKS_PER_BLOCK - 1, -1, -1)
    for j in order:
        cr = slice(j * LIN_CHUNK, (j + 1) * LIN_CHUNK)
        pieces[j] = _dot(q_dec[cr], s.astype(BF16))
        s = s * dec_t[:, j:j + 1] + kv_scr[j]
        yield
    o_dst(rows, o_blk + jnp.concatenate(pieces, axis=0))
    return s


def _run_chains(chains, n_blk):
    def body(i, states):
        gens = [step(i, s) for (step, _), s in zip(chains, states)]
        out = [None] * len(gens)
        live = list(range(len(gens)))
        while live:
            for c in list(live):
                try:
                    next(gens[c])
                except StopIteration as done:
                    out[c] = done.value
                    live.remove(c)
        return tuple(out)
    lax.fori_loop(0, n_blk, body, tuple(jnp.zeros(shape, F32) for _, shape in chains))


def _gated_norm_store(o_scr, gate_ref, gn_ref, out_ref, n_blk, dv):
    heads = o_scr.shape[2] // dv

    def body(i, carry):
        rows = pl.ds(pl.multiple_of(i * ROW_TILE, ROW_TILE), ROW_TILE)
        o = o_scr[0, rows, :] + o_scr[1, rows, :]
        normed = []
        for h in range(heads):
            oh = o[:, h * dv:(h + 1) * dv]
            ms = jnp.mean(oh * oh, axis=-1, keepdims=True)
            normed.append(oh * lax.rsqrt(ms + EPS) * gn_ref[...])
        o = jnp.concatenate(normed, axis=1) if heads > 1 else normed[0]
        gate = gate_ref[0, rows, :].astype(F32)
        out_ref[0, rows, :] = (o * _silu(gate)).astype(out_ref.dtype)
        return carry
    lax.fori_loop(0, n_blk, body, 0)


def _log_sigmoid(v):
    return jnp.minimum(v, 0.0) - jnp.log1p(jnp.exp(-jnp.abs(v)))


def _gla_kernel(q_ref, k_ref, v_ref, gate_ref, lr_ref, gkw_ref, gkb_ref, gn_ref, tri_ref,
                out_ref, g_scr, o_scr, kv_scr, *, n_ctx_blk, dk, dv):
    t_rows = q_ref.shape[1]
    heads = q_ref.shape[2] // dk
    n_blk = t_rows // ROW_TILE
    scale = dk ** -0.5

    def gates(i, carry):
        rows = pl.ds(pl.multiple_of(i * ROW_TILE, ROW_TILE), ROW_TILE)
        lr = lr_ref[0, rows, :]
        for d in range(2):
            lr_d = lr[:, d * GLA_GATE_RANK:(d + 1) * GLA_GATE_RANK].astype(BF16)
            z = _dot(lr_d, gkw_ref[d]) + gkb_ref[d]
            g_scr[d, rows, :] = _log_sigmoid(z) / GLA_GATE_NORM
        return carry
    lax.fori_loop(0, n_blk, gates, 0)

    chains = []
    for h in range(heads):
        kc = slice(h * dk, (h + 1) * dk)
        vc = slice(h * dv, (h + 1) * dv)
        for d in range(2):
            def dst(rows, val, d=d, vc=vc):
                o_scr[d, rows, vc] = val
            def operands(rows, d=d, kc=kc, vc=vc):
                return (q_ref[0, rows, kc].astype(F32) * scale, k_ref[0, rows, kc].astype(F32),
                        v_ref[0, rows, vc], g_scr[d, rows, kc])
            step = functools.partial(
                _scan_block, d, qkvg_of=operands, tri_ref=tri_ref, o_dst=dst, kv_scr=kv_scr.at[len(chains)], n_ctx_blk=n_ctx_blk, n_blk=n_blk)
            chains.append((step, (dk, dv)))
    _run_chains(chains, n_blk)
    _gated_norm_store(o_scr, gate_ref, gn_ref, out_ref, n_blk, dv)


def _hgrn_kernel(q_ref, ff_ref, fb_ref, v_ref, gate_ref, lb_ref, gn_ref, tri_ref,
                 out_ref, o_scr, kv_scr, *, n_ctx_blk, dk):
    t_rows = q_ref.shape[1]
    heads = q_ref.shape[2] // dk
    n_blk = t_rows // ROW_TILE
    f_refs = (ff_ref, fb_ref)

    def forget(d, rows, hc):
        lb = lb_ref[d][:, hc]
        return lb + (1.0 - lb) * jax.nn.sigmoid(f_refs[d][0, rows, hc])

    chains = []
    for h in range(heads):
        hc = slice(h * dk, (h + 1) * dk)
        for d in range(2):
            def dst(rows, val, d=d, hc=hc):
                o_scr[d, rows, hc] = val
            def operands(rows, d=d, hc=hc):
                f = forget(d, rows, hc)
                return q_ref[0, rows, hc].astype(F32), 1.0 - f, v_ref[0, rows, hc], jnp.log(f)
            step = functools.partial(
                _scan_block, d, qkvg_of=operands, tri_ref=tri_ref, o_dst=dst, kv_scr=kv_scr.at[len(chains)], n_ctx_blk=n_ctx_blk, n_blk=n_blk)
            chains.append((step, (dk, dk)))
    _run_chains(chains, n_blk)
    _gated_norm_store(o_scr, gate_ref, gn_ref, out_ref, n_blk, dk)


def _scan_consts():
    r = np.arange(ROW_TILE)
    same = (r[:, None] // LIN_CHUNK) == (r[None, :] // LIN_CHUNK)
    lower = same & (r[None, :] <= r[:, None])
    upper = same & (r[None, :] >= r[:, None])
    return jnp.asarray(np.stack([lower, upper]).astype(np.float32), BF16)


def _even_mixers(p_bf, p_f32, gk_w, gk_b, gla_g, lb, hg_g, lc, cols, d_model):
    b, t, _ = p_bf.shape
    tri = _scan_consts()
    n_ctx_blk = lc // ROW_TILE
    gla_dk = d_model // 2 // GLA_HEADS
    gla_dv = d_model // GLA_HEADS
    hg_heads = d_model // HG_DIM
    const2 = lambda i, h: (0, 0)
    const3 = lambda i, h: (0, 0, 0)

    def head_spec(name, width):
        assert cols[name] % width == 0
        base = cols[name] // width
        return pl.BlockSpec((1, t, width), lambda i, h: (i, 0, base + h))

    gw = GLA_HEADS_PER_STEP
    a_gla = pl.pallas_call(
        functools.partial(_gla_kernel, n_ctx_blk=n_ctx_blk, dk=gla_dk, dv=gla_dv),
        grid=(b, GLA_HEADS // gw),
        in_specs=[head_spec("aq", gw * gla_dk), head_spec("ak", gw * gla_dk), head_spec("av", gw * gla_dv),
                  head_spec("agate", gw * gla_dv),
                  pl.BlockSpec((1, t, LANES), lambda i, h: (i, 0, cols["alr"] // LANES)),
                  pl.BlockSpec((2, GLA_GATE_RANK, gw * gla_dk), lambda i, h: (0, 0, h)),
                  pl.BlockSpec((2, 1, gw * gla_dk), lambda i, h: (0, 0, h)),
                  pl.BlockSpec((1, gla_dv), const2),
                  pl.BlockSpec(tri.shape, const3)],
        out_specs=pl.BlockSpec((1, t, gw * gla_dv), lambda i, h: (i, 0, h)),
        out_shape=jax.ShapeDtypeStruct((b, t, d_model), BF16),
        scratch_shapes=[pltpu.VMEM((2, t, gw * gla_dk), F32), pltpu.VMEM((2, t, gw * gla_dv), F32),
                        pltpu.VMEM((2 * gw, CHUNKS_PER_BLOCK, gla_dk, gla_dv), F32)],
        compiler_params=_cparams("arbitrary", "arbitrary"),
        name="gla_scan",
    )(p_bf, p_bf, p_bf, p_bf, p_f32, gk_w.astype(BF16), gk_b.reshape(2, 1, -1), gla_g.reshape(1, -1), tri)

    hw = HG_HEADS_PER_STEP
    a_hg = pl.pallas_call(
        functools.partial(_hgrn_kernel, n_ctx_blk=n_ctx_blk, dk=HG_DIM),
        grid=(b, hg_heads // hw),
        in_specs=[head_spec("bq", hw * HG_DIM), head_spec("bf0", hw * HG_DIM), head_spec("bf1", hw * HG_DIM),
                  head_spec("bi", hw * HG_DIM), head_spec("bgate", hw * HG_DIM),
                  pl.BlockSpec((2, 1, hw * HG_DIM), lambda i, h: (0, 0, h)),
                  pl.BlockSpec((1, HG_DIM), const2),
                  pl.BlockSpec(tri.shape, const3)],
        out_specs=pl.BlockSpec((1, t, hw * HG_DIM), lambda i, h: (i, 0, h)),
        out_shape=jax.ShapeDtypeStruct((b, t, d_model), BF16),
        scratch_shapes=[pltpu.VMEM((2, t, hw * HG_DIM), F32),
                        pltpu.VMEM((2 * hw, CHUNKS_PER_BLOCK, HG_DIM, HG_DIM), F32)],
        compiler_params=_cparams("arbitrary", "arbitrary"),
        name="hgrn_scan",
    )(p_bf, p_f32, p_f32, p_bf, p_bf, lb.reshape(2, 1, -1), hg_g.reshape(1, -1), tri)
    return a_gla, a_hg


def _even_out_kernel(a_ref, b_ref, wa_ref, wb_ref, mod_ref, *refs, n_ctx_tiles):
    x_refs, o_ref = refs[:-1], refs[-1]
    y = _dot(a_ref[0], wa_ref[...]) + _dot(b_ref[0], wb_ref[...])
    o_ref[0] = _stream_tile(x_refs, n_ctx_tiles) + mod_ref[0, 0, 2:3, :] * y


def _even_out(a_gla, a_hg, w_out, stream, mods, lc):
    parts, b, t, d = _stream_dims(stream)
    wa, wb = w_out[:a_gla.shape[2]].astype(BF16), w_out[a_gla.shape[2]:].astype(BF16)
    row = lambda i, r: (i, r, 0)
    const2 = lambda i, r: (0, 0)
    return pl.pallas_call(
        functools.partial(_even_out_kernel, n_ctx_tiles=lc // ROW_TILE),
        grid=(b, t // ROW_TILE),
        in_specs=[pl.BlockSpec((1, ROW_TILE, a_gla.shape[2]), row),
                  pl.BlockSpec((1, ROW_TILE, a_hg.shape[2]), row),
                  pl.BlockSpec(wa.shape, const2),
                  pl.BlockSpec(wb.shape, const2),
                  pl.BlockSpec((1, 1, 3, d), lambda i, r: (i, jnp.where(r * ROW_TILE >= lc, 1, 0), 0, 0))]
                 + _stream_specs(stream, lc),
        out_specs=pl.BlockSpec((1, ROW_TILE, d), row),
        out_shape=jax.ShapeDtypeStruct((b, t, d), F32),
        compiler_params=_cparams("arbitrary", "arbitrary"),
        name="even_out",
    )(a_gla, a_hg, wa, wb, mods, *parts)


def _rope(xf, cos, sin):
    w = xf.shape[1]
    reps = w // LANES
    cos_w = jnp.concatenate([cos] * reps, axis=1) if reps > 1 else cos
    sin_w = jnp.concatenate([sin] * reps, axis=1) if reps > 1 else sin
    lane = lax.broadcasted_iota(jnp.int32, xf.shape, 1)
    quarter = ATT_HEAD_DIM // 4
    partner = jnp.where(lane % (2 * quarter) < quarter,
                        pltpu.roll(xf, w - quarter, 1), pltpu.roll(xf, quarter, 1))
    return xf * cos_w + partner * sin_w


def _attn_kernel(q_ref, gate_ref, k_ref, v_ref, cq_ref, sq_ref, ck_ref, sk_ref, sink_ref, o_ref,
                 kr_scr, vt_scr, *, lc):
    n = pl.program_id(1)
    t_rows = k_ref.shape[1]
    length = t_rows - lc
    kv_heads = k_ref.shape[2] // ATT_HEAD_DIM
    band = 3 * ATT_BLOCK
    hd = ATT_HEAD_DIM

    @pl.when(n == 0)
    def _():
        def stage(i, carry):
            rows = pl.ds(pl.multiple_of(i * ROW_TILE, ROW_TILE), ROW_TILE)
            kf = k_ref[0, rows, :].astype(F32)
            lat = jnp.maximum(i * ROW_TILE - lc, 0)
            lrows = pl.ds(pl.multiple_of(lat, ROW_TILE), ROW_TILE)
            kr = jnp.where(i * ROW_TILE >= lc, _rope(kf, ck_ref[lrows, :], sk_ref[lrows, :]), kf).astype(BF16)
            vt = v_ref[0, rows, :].astype(F32).T.astype(BF16)
            for hk in range(kv_heads):
                kr_scr[hk, rows, :] = kr[:, hk * hd:(hk + 1) * hd]
                for half in range(ROW_TILE // ATT_BLOCK):
                    blk = i * (ROW_TILE // ATT_BLOCK) + half
                    vt_scr[hk, blk, 0:hd, :] = vt[hk * hd:(hk + 1) * hd, half * ATT_BLOCK:(half + 1) * ATT_BLOCK]
                    vt_scr[hk, blk, hd:2 * hd, :] = jnp.ones((hd, ATT_BLOCK), BF16)
            return carry
        lax.fori_loop(0, t_rows // ROW_TILE, stage, 0)

    start = pl.multiple_of(jnp.clip((n - 1) * ATT_BLOCK, 0, length - band), ATT_BLOCK)
    band_rows = pl.ds(pl.multiple_of(lc + start, ATT_BLOCK), band)
    band_blk = (lc + start) // ATT_BLOCK
    q = _rope(q_ref[0].astype(F32), cq_ref[...], sq_ref[...]) * (hd ** -0.5 * LOG2E)
    q_t = q.T.astype(BF16)
    kpos = start + lax.broadcasted_iota(jnp.int32, (band, ATT_BLOCK), 0)
    qpos = n * ATT_BLOCK + lax.broadcasted_iota(jnp.int32, (band, ATT_BLOCK), 1)
    valid = jnp.abs(qpos - kpos) <= WINDOW
    valid = jnp.concatenate([valid] * ATT_GROUP, axis=1)
    out_rows = []
    for hk in range(kv_heads):
        heads = [hk * ATT_GROUP + g for g in range(ATT_GROUP)]
        q4 = jnp.concatenate([q_t[h * hd:(h + 1) * hd, :] for h in heads], axis=1)
        sink = jnp.concatenate([sink_ref[h:h + 1, :] for h in heads], axis=1) * LOG2E
        s_c = _dot(kr_scr[hk, 0:lc, :], q4)
        s_b = jnp.where(valid, _dot(kr_scr[hk, band_rows, :], q4), NEG_INF)
        m = jnp.maximum(jnp.maximum(jnp.max(s_c, axis=0, keepdims=True),
                                    jnp.max(s_b, axis=0, keepdims=True)), sink)
        p = jnp.concatenate([jnp.exp2(s_c - m), jnp.exp2(s_b - m)], axis=0).astype(BF16)
        vt = jnp.concatenate([vt_scr[hk, c] for c in range(lc // ATT_BLOCK)]
                             + [vt_scr[hk, band_blk + c] for c in range(band // ATT_BLOCK)], axis=1)
        pv = _dot(vt, p)
        o_t = pv[0:hd] / (pv[hd:2 * hd] + jnp.exp2(sink - m))
        out_rows.extend(o_t[:, g * ATT_BLOCK:(g + 1) * ATT_BLOCK] for g in range(ATT_GROUP))
    o = jnp.concatenate(out_rows, axis=0).T
    o_ref[0] = (o * _silu(gate_ref[0].astype(F32))).astype(o_ref.dtype)


def _rope_tables(length, grid_w):
    quarter = ATT_HEAD_DIM // 4
    freqs = ROPE_BASE ** (-jnp.arange(quarter, dtype=F32) / quarter)
    pos = jnp.arange(length)
    ang_r = (pos // grid_w).astype(F32)[:, None] * freqs
    ang_c = (pos % grid_w).astype(F32)[:, None] * freqs
    cos = jnp.concatenate([jnp.cos(ang_r)] * 2 + [jnp.cos(ang_c)] * 2, axis=1)
    sin = jnp.concatenate([-jnp.sin(ang_r), jnp.sin(ang_r), -jnp.sin(ang_c), jnp.sin(ang_c)], axis=1)
    reps = LANES // ATT_HEAD_DIM
    return jnp.tile(cos, (1, reps)), jnp.tile(sin, (1, reps))


def _attention(p1, sink, lc, cols, d_model, grid_w):
    b, t, _ = p1.shape
    length = t - lc
    kv_w = d_model // ATT_GROUP
    assert length % ROW_TILE == 0 and length >= 3 * ATT_BLOCK and lc % ATT_BLOCK == 0
    cos, sin = _rope_tables(length, grid_w)
    sink_b = jnp.broadcast_to(sink.astype(F32)[:, None], (sink.shape[0], LANES))
    q_blk = lc // ATT_BLOCK
    qrow = lambda i, n: (i, q_blk + n, cols["q"] // d_model)
    grow = lambda i, n: (i, q_blk + n, cols["g_att"] // d_model)
    const2 = lambda i, n: (0, 0)
    return pl.pallas_call(
        functools.partial(_attn_kernel, lc=lc),
        grid=(b, length // ATT_BLOCK),
        in_specs=[pl.BlockSpec((1, ATT_BLOCK, d_model), qrow),
                  pl.BlockSpec((1, ATT_BLOCK, d_model), grow),
                  pl.BlockSpec((1, t, kv_w), lambda i, n: (i, 0, cols["k"] // kv_w)),
                  pl.BlockSpec((1, t, kv_w), lambda i, n: (i, 0, cols["v"] // kv_w)),
                  pl.BlockSpec((ATT_BLOCK, LANES), lambda i, n: (n, 0)),
                  pl.BlockSpec((ATT_BLOCK, LANES), lambda i, n: (n, 0)),
                  pl.BlockSpec((length, LANES), const2),
                  pl.BlockSpec((length, LANES), const2),
                  pl.BlockSpec(sink_b.shape, const2)],
        out_specs=pl.BlockSpec((1, ATT_BLOCK, d_model), lambda i, n: (i, n, 0)),
        out_shape=jax.ShapeDtypeStruct((b, length, d_model), BF16),
        scratch_shapes=[pltpu.VMEM((kv_w // ATT_HEAD_DIM, t, ATT_HEAD_DIM), BF16),
                        pltpu.VMEM((kv_w // ATT_HEAD_DIM, t // ATT_BLOCK, 2 * ATT_HEAD_DIM, ATT_BLOCK), BF16)],
        compiler_params=_cparams("arbitrary", "arbitrary"),
        name="window_attn",
    )(p1, p1, p1, p1, cos, sin, cos, sin, sink_b)


def _cmul(ar, ai, br, bi):
    return ar * br - ai * bi, ar * bi + ai * br


def _s5_operators(lam_re, lam_im, log_dt, b_re, b_im, c_re, c_im, d_skip):
    f32 = F32
    lam_re, lam_im, b_re, b_im, c_re, c_im = (a.astype(f32) for a in (lam_re, lam_im, b_re, b_im, c_re, c_im))
    n_groups, n_state, n_ch = b_re.shape
    n_blocks = n_groups // S5_GPB
    dt = jnp.exp(log_dt.astype(f32))[:, :, None]
    steps = jnp.arange(S5_CHUNK + 1, dtype=f32)[:, None, None, None]
    mag = jnp.exp(lam_re * dt * steps)
    pw_re, pw_im = mag * jnp.cos(lam_im * dt * steps), mag * jnp.sin(lam_im * dt * steps)
    a_re, a_im = pw_re[1], pw_im[1]
    inv = 1.0 / (lam_re * lam_re + lam_im * lam_im)
    co_re, co_im = _cmul(a_re - 1.0, a_im, lam_re * inv, -lam_im * inv)
    bb_re, bb_im = _cmul(co_re[..., None], co_im[..., None], b_re[None], b_im[None])
    e_re, e_im = _cmul(pw_re[..., None], pw_im[..., None], bb_re[None], bb_im[None])
    lag = S5_CHUNK
    k_lag = (jnp.einsum("gop,tdgpi->tdgoi", c_re, e_re[:lag], precision=HIGHEST)
             - jnp.einsum("gop,tdgpi->tdgoi", c_im, e_im[:lag], precision=HIGHEST))
    diag = k_lag[0, 0] + k_lag[0, 1] + d_skip.astype(f32)[:, :, None] * jnp.eye(n_ch, dtype=f32)
    by_lag = jnp.concatenate([k_lag[1:, 1][::-1], diag[None], k_lag[1:, 0]], axis=0)
    n_lags = 2 * lag - 1
    state_w = 4 * S5_GPB * n_state
    lag5 = by_lag.reshape(n_lags, n_blocks, S5_GPB, n_ch, n_ch)
    toep_c = lag5.transpose(1, 0, 2, 4, 3).reshape(n_blocks, n_lags * LANES, n_ch)
    n_re = jnp.stack([e_re[:lag][::-1, 0], e_re[:lag, 1]])
    n_im = jnp.stack([e_im[:lag][::-1, 0], e_im[:lag, 1]])
    n7 = jnp.stack([n_re, n_im], axis=2).reshape(2, lag, 2, n_blocks, S5_GPB, n_state, n_ch)
    state_c = n7.transpose(3, 1, 6, 0, 2, 4, 5).reshape(n_blocks, lag * n_ch, state_w)
    r_re = jnp.stack([pw_re[1:, 0], pw_re[1:, 1][::-1]])
    r_im = jnp.stack([pw_im[1:, 0], pw_im[1:, 1][::-1]])
    ca_re, ca_im = _cmul(c_re[None, None], c_im[None, None],
                         r_re[:, :, :, None, :], r_im[:, :, :, None, :])
    m7 = jnp.stack([ca_re, -ca_im], axis=1).reshape(2, 2, lag, n_blocks, S5_GPB, n_ch, n_state)
    read_c = m7.transpose(3, 0, 1, 4, 6, 2, 5).reshape(n_blocks, state_w, lag * n_ch)
    dec = jnp.stack([pw_re[lag], pw_im[lag]], axis=1)
    dec = dec.reshape(2, 2, n_blocks, S5_GPB * n_state).transpose(2, 0, 1, 3)
    dec = dec.reshape(n_blocks, 2, 1, 2 * S5_GPB * n_state)
    c = np.arange(S5_FOLD)
    src = (c // LANES) * n_ch + c % n_ch
    spread = jnp.asarray((np.arange(lag * n_ch)[:, None] == src[None, :]).astype(np.float32), BF16)
    return toep_c.astype(BF16), state_c.astype(BF16), read_c.astype(BF16), dec, spread


def _group_of(shape, axis, span):
    return (lax.broadcasted_iota(jnp.int32, shape, axis) // span) % S5_GPB


def _folded_rows(u_ref):
    return jnp.concatenate([u_ref[0, t] for t in range(S5_CHUNK)], axis=1)


def _s5_local_state_kernel(u_ref, sc_ref, spread_ref, o_ref, w_scr):
    @pl.when(pl.program_id(1) == 0)
    def _():
        n_state = sc_ref.shape[2] // (4 * S5_GPB)
        for m in range(S5_FOLD // ROW_TILE):
            rows = slice(m * ROW_TILE, (m + 1) * ROW_TILE)
            full = _dot_tn(spread_ref[:, rows], sc_ref[0])
            keep = _group_of(full.shape, 0, S5_GROUP_CH) == _group_of(full.shape, 1, n_state)
            w_scr[rows, :] = jnp.where(keep, full, 0.0).astype(BF16)
    o_ref[0] = _dot(_folded_rows(u_ref), w_scr[...])


def _s5_state_kernel(xl_ref, dec_ref, xp_ref, *, batch, n_ctx_chunks):
    d = pl.program_id(1)
    n_chunks = xl_ref.shape[1] // batch
    half = xl_ref.shape[2] // 2
    a_re, a_im = dec_ref[0, 0, :, :half], dec_ref[0, 0, :, half:]

    def step(i, state):
        s_re, s_im = state
        c_bwd = jnp.where(i < n_ctx_chunks, n_ctx_chunks - 1 - i, n_chunks - 1 - (i - n_ctx_chunks))
        c = jnp.where(d == 0, i, c_bwd)
        rows = pl.ds(pl.multiple_of(c * batch, batch), batch)
        xp_ref[0, rows, :] = jnp.concatenate([s_re, s_im], axis=1).astype(xp_ref.dtype)
        xl = xl_ref[0, rows, :]
        return (a_re * s_re - a_im * s_im + xl[:, :half], a_re * s_im + a_im * s_re + xl[:, half:])

    zero = jnp.zeros((batch, half), F32)
    lax.fori_loop(0, n_chunks, step, (zero, zero))


def _s5_out_kernel(u_ref, xp_ref, tc_ref, rc_ref, spread_ref, o_ref, lag_scr, toep_scr, read_scr):
    @pl.when(pl.program_id(1) == 0)
    def _():
        n_state = rc_ref.shape[1] // (4 * S5_GPB)
        n_ch = tc_ref.shape[2]
        full = _dot(tc_ref[0], spread_ref[0:n_ch, 0:LANES])
        keep = _group_of(full.shape, 0, S5_GROUP_CH) == _group_of(full.shape, 1, S5_GROUP_CH)
        lag_scr[...] = jnp.where(keep, full, 0.0).astype(BF16)
        for t_in in range(S5_CHUNK):
            for t_out in range(S5_CHUNK):
                lag = t_out - t_in + S5_CHUNK - 1
                toep_scr[t_in * LANES:(t_in + 1) * LANES, t_out * LANES:(t_out + 1) * LANES] = (
                    lag_scr[lag * LANES:(lag + 1) * LANES, :])
        for m in range(S5_FOLD // ROW_TILE):
            cols = slice(m * ROW_TILE, (m + 1) * ROW_TILE)
            full = _dot(rc_ref[0], spread_ref[:, cols])
            keep = _group_of(full.shape, 0, n_state) == _group_of(full.shape, 1, S5_GROUP_CH)
            read_scr[:, cols] = jnp.where(keep, full, 0.0).astype(BF16)
    y = _dot(_folded_rows(u_ref), toep_scr[...]) + _dot(xp_ref[0], read_scr[...])
    for t in range(S5_CHUNK):
        o_ref[0, t] = y[:, t * LANES:(t + 1) * LANES]


def _s5(u, lc, ops):
    toep_c, state_c, read_c, dec, spread = ops
    b, t, w = u.shape
    n_blocks = w // LANES
    n_chunks, n_ctx_chunks = t // S5_CHUNK, lc // S5_CHUNK
    rows = n_chunks * b
    state_w = state_c.shape[2]
    assert b % 16 == 0 and rows % ROW_TILE == 0 and (n_ctx_chunks * b) % ROW_TILE == 0
    u2 = u.reshape(b, n_chunks, S5_CHUNK, n_blocks, LANES).transpose(3, 2, 1, 0, 4)
    u2 = u2.reshape(n_blocks, S5_CHUNK, rows, LANES)
    blk_row = lambda g, r: (g, r, 0)
    blk_w = lambda g, r: (g, 0, 0)
    const2 = lambda g, r: (0, 0)
    x_loc = pl.pallas_call(
        _s5_local_state_kernel,
        grid=(n_blocks, rows // ROW_TILE),
        in_specs=[pl.BlockSpec((1, S5_CHUNK, ROW_TILE, LANES), lambda g, r: (g, 0, r, 0)),
                  pl.BlockSpec((1,) + state_c.shape[1:], blk_w),
                  pl.BlockSpec(spread.shape, const2)],
        out_specs=pl.BlockSpec((1, ROW_TILE, state_w), blk_row),
        out_shape=jax.ShapeDtypeStruct((n_blocks, rows, state_w), F32),
        scratch_shapes=[pltpu.VMEM((S5_FOLD, state_w), BF16)],
        compiler_params=_cparams("arbitrary", "arbitrary"),
        name="s5_local_state",
    )(u2, state_c, spread)
    x_prev = pl.pallas_call(
        functools.partial(_s5_state_kernel, batch=b, n_ctx_chunks=n_ctx_chunks),
        grid=(n_blocks, 2),
        in_specs=[pl.BlockSpec((1, rows, state_w // 2), lambda g, d: (g, 0, d)),
                  pl.BlockSpec((1, 1, 1, state_w // 2), lambda g, d: (g, d, 0, 0))],
        out_specs=pl.BlockSpec((1, rows, state_w // 2), lambda g, d: (g, 0, d)),
        out_shape=jax.ShapeDtypeStruct((n_blocks, rows, state_w), BF16),
        compiler_params=_cparams("arbitrary", "arbitrary"),
        name="s5_state_scan",
    )(x_loc, dec)
    ctx_tiles = n_ctx_chunks * b // ROW_TILE
    lat_rows = rows - n_ctx_chunks * b
    lat_row = lambda g, r: (g, ctx_tiles + r, 0)
    y2 = pl.pallas_call(
        _s5_out_kernel,
        grid=(n_blocks, lat_rows // ROW_TILE),
        in_specs=[pl.BlockSpec((1, S5_CHUNK, ROW_TILE, LANES), lambda g, r: (g, 0, ctx_tiles + r, 0)),
                  pl.BlockSpec((1, ROW_TILE, state_w), lat_row),
                  pl.BlockSpec((1,) + toep_c.shape[1:], blk_w),
                  pl.BlockSpec((1,) + read_c.shape[1:], blk_w),
                  pl.BlockSpec(spread.shape, const2)],
        out_specs=pl.BlockSpec((1, S5_CHUNK, ROW_TILE, LANES), lambda g, r: (g, 0, r, 0)),
        out_shape=jax.ShapeDtypeStruct((n_blocks, S5_CHUNK, lat_rows, LANES), F32),
        scratch_shapes=[pltpu.VMEM(toep_c.shape[1:2] + (LANES,), BF16),
                        pltpu.VMEM((S5_FOLD, S5_FOLD), BF16), pltpu.VMEM((state_w, S5_FOLD), BF16)],
        compiler_params=_cparams("arbitrary", "arbitrary"),
        name="s5_chunk_out",
    )(u2, x_prev, toep_c, read_c, spread)
    y = y2.reshape(n_blocks, S5_CHUNK, n_chunks - n_ctx_chunks, b, LANES).transpose(0, 3, 2, 1, 4)
    return y.reshape(n_blocks, b, t - lc, LANES)


def _odd_out_kernel(att_ref, y_ref, gs_ref, glu_ref, wa_ref, wb_ref, x_ref, mod_ref, fg_ref, o_ref):
    y_s5 = jnp.concatenate([y_ref[g, 0] for g in range(y_ref.shape[0])], axis=1)
    width = y_s5.shape[1]
    z = jax.nn.gelu(y_s5).astype(BF16)
    ab = _dot(z, glu_ref[...])
    s5 = ab[:, :width] * jax.nn.sigmoid(ab[:, width:]) * _silu(gs_ref[0].astype(F32))
    y = _dot(att_ref[0], wa_ref[...]) + _dot(s5.astype(BF16), wb_ref[...])
    xo = x_ref[0] + mod_ref[0, 0, 2:3, :] * y
    ms = jnp.mean(xo * xo, axis=-1, keepdims=True)
    o_ref[0] = xo * lax.rsqrt(ms + EPS) * fg_ref[...]


def _odd_out(att, y_s5, p1, glu_w, w_out, xs, mods, final_g, lc, cols):
    b, length, d = att.shape
    wa, wb = w_out[:d].astype(BF16), w_out[d:].astype(BF16)
    lat = lc // ROW_TILE
    row = lambda i, r: (i, r, 0)
    const2 = lambda i, r: (0, 0)
    return pl.pallas_call(
        _odd_out_kernel,
        grid=(b, length // ROW_TILE),
        in_specs=[pl.BlockSpec((1, ROW_TILE, d), row),
                  pl.BlockSpec((y_s5.shape[0], 1, ROW_TILE, LANES), lambda i, r: (0, i, r, 0)),
                  pl.BlockSpec((1, ROW_TILE, d), lambda i, r: (i, lat + r, cols["g_s5"] // d)),
                  pl.BlockSpec(glu_w.shape, const2),
                  pl.BlockSpec(wa.shape, const2),
                  pl.BlockSpec(wb.shape, const2),
                  pl.BlockSpec((1, ROW_TILE, d), lambda i, r: (i, lat + r, 0)),
                  pl.BlockSpec((1, 1, 3, d), lambda i, r: (i, 1, 0, 0)),
                  pl.BlockSpec((1, d), const2)],
        out_specs=pl.BlockSpec((1, ROW_TILE, d), row),
        out_shape=jax.ShapeDtypeStruct((b, length, d), F32),
        compiler_params=_cparams("arbitrary", "arbitrary"),
        name="odd_out",
    )(att, y_s5, p1, glu_w.astype(BF16), wa, wb, xs, mods, final_g.reshape(1, d))


def _offsets(names_widths):
    cols, off = {}, 0
    for name, width in names_widths:
        cols[name] = off
        off += width
    return cols, off


def kernel(x, c, ctx, c_ctx, ada_w, ada_b, norm_g, final_norm_g, ev_w_in, ev_w_out, gla_gk_w, gla_gk_b,
           gla_norm_g, hgrn_lb_raw, hgrn_norm_g, od_w_in, od_w_out, attn_sink, s5_lambda_re, s5_lambda_im,
           s5_log_dt, s5_b_re, s5_b_im, s5_c_re, s5_c_im, s5_d, s5_glu_w):
    b, length, d = x.shape
    lc = ctx.shape[1]
    assert ada_w.shape[0] == 2 and ev_w_in.shape[0] == 1 and od_w_in.shape[0] == 1
    grid_w = 64
    xs = (ctx, x)

    pad = (-(b + 1)) % 8
    cc = jnp.concatenate([c, c_ctx[None], jnp.zeros((pad, d), c.dtype)], axis=0)
    mods_all = _ada_mods(cc, ada_w, ada_b)

    def layer_mods(layer):
        lat = mods_all[layer, :b].reshape(b, 1, 3, d)
        cx = jnp.broadcast_to(mods_all[layer, b].reshape(1, 1, 3, d), (b, 1, 3, d))
        return jnp.concatenate([cx, lat], axis=1)

    mods0 = layer_mods(0)
    half = d // 2
    w0 = ev_w_in[0]
    src, _ = _offsets([("aq", half), ("ak", half), ("av", d), ("alr", 2 * GLA_GATE_RANK), ("agate", d),
                       ("bq", d), ("bf", 2 * d), ("bi", d), ("bgate", d)])
    bf_names = [("aq", half), ("ak", half), ("av", d), ("agate", d), ("bq", d), ("bi", d), ("bgate", d)]
    cols_bf, _ = _offsets(bf_names)
    w0_bf = jnp.concatenate([w0[:, src[n]:src[n] + wd] for n, wd in bf_names], axis=1).astype(BF16)
    lr_pad = ROW_TILE - 2 * GLA_GATE_RANK
    cols_f32, _ = _offsets([("bf0", d), ("bf1", d), ("alr", ROW_TILE)])
    w0_f32 = jnp.concatenate([w0[:, src["bf"]:src["bf"] + 2 * d],
                              w0[:, src["alr"]:src["alr"] + 2 * GLA_GATE_RANK],
                              jnp.zeros((d, lr_pad), w0.dtype)], axis=1).astype(BF16)
    p_bf, p_f32 = _inproj(xs, mods0, norm_g[0], [w0_bf, w0_f32], [BF16, F32], lc)
    lb_all = jnp.cumsum(jax.nn.softmax(hgrn_lb_raw.astype(F32), axis=1), axis=1)
    cols0 = dict(cols_bf, **cols_f32)
    a_gla, a_hg = _even_mixers(p_bf, p_f32, gla_gk_w[0], gla_gk_b[0], gla_norm_g[0], lb_all[:, 0],
                               hgrn_norm_g[0], lc, cols0, d)
    xs1 = _even_out(a_gla, a_hg, ev_w_out[0], xs, mods0, lc)

    mods1 = layer_mods(1)
    kv_w = d // ATT_GROUP
    w1 = od_w_in[0]
    src1, _ = _offsets([("q", d), ("k", kv_w), ("v", kv_w), ("g_att", d), ("u", d), ("g_s5", d)])
    names1 = [("q", d), ("g_att", d), ("g_s5", d), ("k", kv_w), ("v", kv_w)]
    cols1, _ = _offsets(names1)
    w1_bf = jnp.concatenate([w1[:, src1[n]:src1[n] + wd] for n, wd in names1], axis=1).astype(BF16)
    w1_u = w1[:, src1["u"]:src1["u"] + d].astype(BF16)
    p1, u = _inproj(xs1, mods1, norm_g[1], [w1_bf, w1_u], [BF16, BF16], lc)
    att = _attention(p1, attn_sink[0], lc, cols1, d, grid_w)
    ops = _s5_operators(s5_lambda_re[0], s5_lambda_im[0], s5_log_dt[0], s5_b_re[0], s5_b_im[0],
                        s5_c_re[0], s5_c_im[0], s5_d[0])
    y_s5 = _s5(u, lc, ops)
    return _odd_out(att, y_s5, p1, s5_glu_w[0], od_w_out[0], xs1, mods1, final_norm_g, lc, cols1)
```

```python
import functools
import math

import jax
import jax.numpy as jnp
import numpy as np
from jax import lax
from jax.experimental import pallas as pl
from jax.experimental.pallas import tpu as pltpu

EPS = 1e-6
LANES = 128
ROW_TILE = 256
PROJ_COLS = 512
LIN_CHUNK = 32
CHUNKS_PER_BLOCK = ROW_TILE // LIN_CHUNK
VMEM_LIMIT = 56 * 1024 * 1024

GLA_HEADS = 4
GLA_HEADS_PER_STEP = 2
HG_HEADS_PER_STEP = 4
GLA_GATE_RANK = 16
GLA_GATE_NORM = 16.0
HG_DIM = 128
ATT_HEAD_DIM = 64
ATT_GROUP = 4
WINDOW = 128
ATT_BLOCK = 128
ROPE_BASE = 10000.0
NEG_INF = -1e30
LOG2E = 1.4426950408889634
S5_GROUP_CH = 16
S5_STATE = 64
S5_CHUNK = 16
S5_GPB = LANES // S5_GROUP_CH
S5_FOLD = S5_CHUNK * LANES

F32 = jnp.float32
BF16 = jnp.bfloat16
HIGHEST = lax.Precision.HIGHEST


def _cparams(*sem):
    return pltpu.CompilerParams(dimension_semantics=sem, vmem_limit_bytes=VMEM_LIMIT)


def _dot(a, b):
    return jnp.dot(a, b, preferred_element_type=F32)


def _dot_nt(a, b):
    return lax.dot_general(a, b, (((1,), (1,)), ((), ())), preferred_element_type=F32)


def _dot_tn(a, b):
    return lax.dot_general(a, b, (((0,), (0,)), ((), ())), preferred_element_type=F32)


def _silu(v):
    return v * jax.nn.sigmoid(v)


def _split_hi_lo(v):
    hi = v.astype(BF16)
    lo = (v - hi.astype(F32)).astype(BF16)
    return hi, lo


def _ada_kernel(c_ref, w_ref, b_ref, o_ref):
    o_ref[0] = _dot(_silu(c_ref[...]).astype(BF16), w_ref[0]) + b_ref[0]


def _ada_mods(cc, ada_w, ada_b, tn=512):
    depth, d, n = ada_w.shape
    rows = cc.shape[0]
    return pl.pallas_call(
        _ada_kernel,
        grid=(depth, n // tn),
        in_specs=[pl.BlockSpec((rows, d), lambda l, j: (0, 0)),
                  pl.BlockSpec((1, d, tn), lambda l, j: (l, 0, j)),
                  pl.BlockSpec((1, 1, tn), lambda l, j: (l, 0, j))],
        out_specs=pl.BlockSpec((1, rows, tn), lambda l, j: (l, 0, j)),
        out_shape=jax.ShapeDtypeStruct((depth, rows, n), F32),
        compiler_params=_cparams("arbitrary", "arbitrary"),
        name="ada_mods",
    )(cc, ada_w.astype(BF16), ada_b.reshape(depth, 1, n))


def _stream_specs(stream, lc):
    if not isinstance(stream, tuple):
        return [pl.BlockSpec((1, ROW_TILE, stream.shape[2]), lambda i, r: (i, r, 0))]
    n_ctx = lc // ROW_TILE
    d = stream[0].shape[2]
    return [pl.BlockSpec((1, ROW_TILE, d), lambda i, r: (i, jnp.minimum(r, n_ctx - 1), 0)),
            pl.BlockSpec((1, ROW_TILE, d), lambda i, r: (i, jnp.maximum(r - n_ctx, 0), 0))]


def _stream_tile(x_refs, n_ctx_tiles):
    if len(x_refs) == 1:
        return x_refs[0][0]
    return jnp.where(pl.program_id(1) < n_ctx_tiles, x_refs[0][0], x_refs[1][0])


def _inproj_kernel(*refs, n_stream, n_ctx_tiles):
    x_refs, (mod_ref, g_ref), refs = refs[:n_stream], refs[n_stream:n_stream + 2], refs[n_stream + 2:]
    n_out = len(refs) // 2
    w_refs, o_refs = refs[:n_out], refs[n_out:]
    xf = _stream_tile(x_refs, n_ctx_tiles)
    ms = jnp.mean(xf * xf, axis=-1, keepdims=True)
    h = (xf * lax.rsqrt(ms + EPS) * g_ref[...] * (1.0 + mod_ref[0, 0, 1:2, :]) + mod_ref[0, 0, 0:1, :]).astype(BF16)
    for w_ref, o_ref in zip(w_refs, o_refs):
        width = w_ref.shape[1]
        step = math.gcd(width, PROJ_COLS)
        for c in range(0, width, step):
            o_ref[0, :, c:c + step] = _dot(h, w_ref[:, c:c + step]).astype(o_ref.dtype)


def _stream_dims(stream):
    parts = stream if isinstance(stream, tuple) else (stream,)
    return parts, parts[0].shape[0], sum(p.shape[1] for p in parts), parts[0].shape[2]


def _inproj(stream, mods, g, weights, out_dtypes, lc):
    parts, b, t, d = _stream_dims(stream)
    assert t % ROW_TILE == 0 and lc % ROW_TILE == 0
    row = lambda i, r: (i, r, 0)
    const2 = lambda i, r: (0, 0)
    return pl.pallas_call(
        functools.partial(_inproj_kernel, n_stream=len(parts), n_ctx_tiles=lc // ROW_TILE),
        grid=(b, t // ROW_TILE),
        in_specs=_stream_specs(stream, lc)
                 + [pl.BlockSpec((1, 1, 3, d), lambda i, r: (i, jnp.where(r * ROW_TILE >= lc, 1, 0), 0, 0)),
                    pl.BlockSpec((1, d), const2)]
                 + [pl.BlockSpec(w.shape, const2) for w in weights],
        out_specs=[pl.BlockSpec((1, ROW_TILE, w.shape[1]), row) for w in weights],
        out_shape=[jax.ShapeDtypeStruct((b, t, w.shape[1]), dt) for w, dt in zip(weights, out_dtypes)],
        compiler_params=_cparams("arbitrary", "arbitrary"),
        name="inproj",
    )(*parts, mods, g.reshape(1, d), *weights)


def _scan_block(d, i, s, qkvg_of, tri_ref, o_dst, kv_scr, *, n_ctx_blk, n_blk):
    tri = tri_ref[d]
    if d == 0:
        blk = i
    else:
        blk = jnp.where(i < n_ctx_blk, n_ctx_blk - 1 - i, n_blk - 1 - (i - n_ctx_blk))
    rows = pl.ds(pl.multiple_of(blk * ROW_TILE, ROW_TILE), ROW_TILE)
    q, k, v, g = qkvg_of(rows)
    dk = g.shape[1]
    g_hi, g_lo = _split_hi_lo(g)
    cs = _dot(tri, jnp.concatenate([g_hi, g_lo], axis=1))
    yield
    bcum = cs[:, :dk] + cs[:, dk:]
    last = LIN_CHUNK - 1 if d == 0 else 0
    tot3 = bcum.reshape(CHUNKS_PER_BLOCK, LIN_CHUNK, dk)[:, last:last + 1, :]
    total = jnp.broadcast_to(tot3, (CHUNKS_PER_BLOCK, LIN_CHUNK, dk)).reshape(ROW_TILE, dk)
    q_dec = (q * jnp.exp(bcum)).astype(BF16)
    k_inv = (k * jnp.exp(-bcum)).astype(BF16)
    k_tail = (k * jnp.exp(total - bcum)).astype(BF16)
    scores = _dot_nt(q_dec, k_inv)
    yield
    for j in range(CHUNKS_PER_BLOCK):
        cr = slice(j * LIN_CHUNK, (j + 1) * LIN_CHUNK)
        kv_scr[j] = _dot_tn(k_tail[cr], v[cr])
    tot8 = jnp.concatenate([tot3.reshape(CHUNKS_PER_BLOCK, dk),
                            jnp.zeros((LANES - CHUNKS_PER_BLOCK, dk), F32)], axis=0)
    dec_t = jnp.exp(tot8.T)
    yield
    att = jnp.where(tri > 0, scores, 0.0).astype(BF16)
    o_blk = _dot(att, v)
    yield
    pieces = [None] * CHUNKS_PER_BLOCK
    order = range(CHUNKS_PER_BLOCK) if d == 0 else range(CHUNKS_PER_BLOCK - 1, -1, -1)
    for j in order:
        cr = slice(j * LIN_CHUNK, (j + 1) * LIN_CHUNK)
        pieces[j] = _dot(q_dec[cr], s.astype(BF16))
        s = s * dec_t[:, j:j + 1] + kv_scr[j]
        yield
    o_dst(rows, o_blk + jnp.concatenate(pieces, axis=0))
    return s


def _run_chains(chains, n_blk):
    def body(i, states):
        gens = [step(i, s) for (step, _), s in zip(chains, states)]
        out = [None] * len(gens)
        live = list(range(len(gens)))
        while live:
            for c in list(live):
                try:
                    next(gens[c])
                except StopIteration as done:
                    out[c] = done.value
                    live.remove(c)
        return tuple(out)
    lax.fori_loop(0, n_blk, body, tuple(jnp.zeros(shape, F32) for _, shape in chains))


def _gated_norm_store(o_scr, gate_ref, gn_ref, out_ref, n_blk, dv):
    heads = o_scr.shape[2] // dv

    def body(i, carry):
        rows = pl.ds(pl.multiple_of(i * ROW_TILE, ROW_TILE), ROW_TILE)
        o = o_scr[0, rows, :] + o_scr[1, rows, :]
        normed = []
        for h in range(heads):
            oh = o[:, h * dv:(h + 1) * dv]
            ms = jnp.mean(oh * oh, axis=-1, keepdims=True)
            normed.append(oh * lax.rsqrt(ms + EPS) * gn_ref[...])
        o = jnp.concatenate(normed, axis=1) if heads > 1 else normed[0]
        gate = gate_ref[0, rows, :].astype(F32)
        out_ref[0, rows, :] = (o * _silu(gate)).astype(out_ref.dtype)
        return carry
    lax.fori_loop(0, n_blk, body, 0)


def _log_sigmoid(v):
    return jnp.minimum(v, 0.0) - jnp.log1p(jnp.exp(-jnp.abs(v)))


def _gla_kernel(q_ref, k_ref, v_ref, gate_ref, lr_ref, gkw_ref, gkb_ref, gn_ref, tri_ref,
                out_ref, g_scr, o_scr, kv_scr, *, n_ctx_blk, dk, dv):
    t_rows = q_ref.shape[1]
    heads = q_ref.shape[2] // dk
    n_blk = t_rows // ROW_TILE
    scale = dk ** -0.5

    def gates(i, carry):
        rows = pl.ds(pl.multiple_of(i * ROW_TILE, ROW_TILE), ROW_TILE)
        lr = lr_ref[0, rows, :]
        for d in range(2):
            lr_d = lr[:, d * GLA_GATE_RANK:(d + 1) * GLA_GATE_RANK].astype(BF16)
            z = _dot(lr_d, gkw_ref[d]) + gkb_ref[d]
            g_scr[d, rows, :] = _log_sigmoid(z) / GLA_GATE_NORM
        return carry
    lax.fori_loop(0, n_blk, gates, 0)

    chains = []
    for h in range(heads):
        kc = slice(h * dk, (h + 1) * dk)
        vc = slice(h * dv, (h + 1) * dv)
        for d in range(2):
            def dst(rows, val, d=d, vc=vc):
                o_scr[d, rows, vc] = val
            def operands(rows, d=d, kc=kc, vc=vc):
                return (q_ref[0, rows, kc].astype(F32) * scale, k_ref[0, rows, kc].astype(F32),
                        v_ref[0, rows, vc], g_scr[d, rows, kc])
            step = functools.partial(
                _scan_block, d, qkvg_of=operands, tri_ref=tri_ref, o_dst=dst, kv_scr=kv_scr.at[len(chains)], n_ctx_blk=n_ctx_blk, n_blk=n_blk)
            chains.append((step, (dk, dv)))
    _run_chains(chains, n_blk)
    _gated_norm_store(o_scr, gate_ref, gn_ref, out_ref, n_blk, dv)


def _hgrn_kernel(q_ref, ff_ref, fb_ref, v_ref, gate_ref, lb_ref, gn_ref, tri_ref,
                 out_ref, o_scr, kv_scr, *, n_ctx_blk, dk):
    t_rows = q_ref.shape[1]
    heads = q_ref.shape[2] // dk
    n_blk = t_rows // ROW_TILE
    f_refs = (ff_ref, fb_ref)

    def forget(d, rows, hc):
        lb = lb_ref[d][:, hc]
        return lb + (1.0 - lb) * jax.nn.sigmoid(f_refs[d][0, rows, hc])

    chains = []
    for h in range(heads):
        hc = slice(h * dk, (h + 1) * dk)
        for d in range(2):
            def dst(rows, val, d=d, hc=hc):
                o_scr[d, rows, hc] = val
            def operands(rows, d=d, hc=hc):
                f = forget(d, rows, hc)
                return q_ref[0, rows, hc].astype(F32), 1.0 - f, v_ref[0, rows, hc], jnp.log(f)
            step = functools.partial(
                _scan_block, d, qkvg_of=operands, tri_ref=tri_ref, o_dst=dst, kv_scr=kv_scr.at[len(chains)], n_ctx_blk=n_ctx_blk, n_blk=n_blk)
            chains.append((step, (dk, dk)))
    _run_chains(chains, n_blk)
    _gated_norm_store(o_scr, gate_ref, gn_ref, out_ref, n_blk, dk)


def _scan_consts():
    r = np.arange(ROW_TILE)
    same = (r[:, None] // LIN_CHUNK) == (r[None, :] // LIN_CHUNK)
    lower = same & (r[None, :] <= r[:, None])
    upper = same & (r[None, :] >= r[:, None])
    return jnp.asarray(np.stack([lower, upper]).astype(np.float32), BF16)


def _even_mixers(p_bf, p_f32, gk_w, gk_b, gla_g, lb, hg_g, lc, cols, d_model):
    b, t, _ = p_bf.shape
    tri = _scan_consts()
    n_ctx_blk = lc // ROW_TILE
    gla_dk = d_model // 2 // GLA_HEADS
    gla_dv = d_model // GLA_HEADS
    hg_heads = d_model // HG_DIM
    const2 = lambda i, h: (0, 0)
    const3 = lambda i, h: (0, 0, 0)

    def head_spec(name, width):
        assert cols[name] % width == 0
        base = cols[name] // width
        return pl.BlockSpec((1, t, width), lambda i, h: (i, 0, base + h))

    gw = GLA_HEADS_PER_STEP
    a_gla = pl.pallas_call(
        functools.partial(_gla_kernel, n_ctx_blk=n_ctx_blk, dk=gla_dk, dv=gla_dv),
        grid=(b, GLA_HEADS // gw),
        in_specs=[head_spec("aq", gw * gla_dk), head_spec("ak", gw * gla_dk), head_spec("av", gw * gla_dv),
                  head_spec("agate", gw * gla_dv),
                  pl.BlockSpec((1, t, LANES), lambda i, h: (i, 0, cols["alr"] // LANES)),
                  pl.BlockSpec((2, GLA_GATE_RANK, gw * gla_dk), lambda i, h: (0, 0, h)),
                  pl.BlockSpec((2, 1, gw * gla_dk), lambda i, h: (0, 0, h)),
                  pl.BlockSpec((1, gla_dv), const2),
                  pl.BlockSpec(tri.shape, const3)],
        out_specs=pl.BlockSpec((1, t, gw * gla_dv), lambda i, h: (i, 0, h)),
        out_shape=jax.ShapeDtypeStruct((b, t, d_model), BF16),
        scratch_shapes=[pltpu.VMEM((2, t, gw * gla_dk), F32), pltpu.VMEM((2, t, gw * gla_dv), F32),
                        pltpu.VMEM((2 * gw, CHUNKS_PER_BLOCK, gla_dk, gla_dv), F32)],
        compiler_params=_cparams("arbitrary", "arbitrary"),
        name="gla_scan",
    )(p_bf, p_bf, p_bf, p_bf, p_f32, gk_w.astype(BF16), gk_b.reshape(2, 1, -1), gla_g.reshape(1, -1), tri)

    hw = HG_HEADS_PER_STEP
    a_hg = pl.pallas_call(
        functools.partial(_hgrn_kernel, n_ctx_blk=n_ctx_blk, dk=HG_DIM),
        grid=(b, hg_heads // hw),
        in_specs=[head_spec("bq", hw * HG_DIM), head_spec("bf0", hw * HG_DIM), head_spec("bf1", hw * HG_DIM),
                  head_spec("bi", hw * HG_DIM), head_spec("bgate", hw * HG_DIM),
                  pl.BlockSpec((2, 1, hw * HG_DIM), lambda i, h: (0, 0, h)),
                  pl.BlockSpec((1, HG_DIM), const2),
                  pl.BlockSpec(tri.shape, const3)],
        out_specs=pl.BlockSpec((1, t, hw * HG_DIM), lambda i, h: (i, 0, h)),
        out_shape=jax.ShapeDtypeStruct((b, t, d_model), BF16),
        scratch_shapes=[pltpu.VMEM((2, t, hw * HG_DIM), F32),
                        pltpu.VMEM((2 * hw, CHUNKS_PER_BLOCK, HG_DIM, HG_DIM), F32)],
        compiler_params=_cparams("arbitrary", "arbitrary"),
        name="hgrn_scan",
    )(p_bf, p_f32, p_f32, p_bf, p_bf, lb.reshape(2, 1, -1), hg_g.reshape(1, -1), tri)
    return a_gla, a_hg


def _even_out_kernel(a_ref, b_ref, wa_ref, wb_ref, mod_ref, *refs, n_ctx_tiles):
    x_refs, o_ref = refs[:-1], refs[-1]
    y = _dot(a_ref[0], wa_ref[...]) + _dot(b_ref[0], wb_ref[...])
    o_ref[0] = _stream_tile(x_refs, n_ctx_tiles) + mod_ref[0, 0, 2:3, :] * y


def _even_out(a_gla, a_hg, w_out, stream, mods, lc):
    parts, b, t, d = _stream_dims(stream)
    wa, wb = w_out[:a_gla.shape[2]].astype(BF16), w_out[a_gla.shape[2]:].astype(BF16)
    row = lambda i, r: (i, r, 0)
    const2 = lambda i, r: (0, 0)
    return pl.pallas_call(
        functools.partial(_even_out_kernel, n_ctx_tiles=lc // ROW_TILE),
        grid=(b, t // ROW_TILE),
        in_specs=[pl.BlockSpec((1, ROW_TILE, a_gla.shape[2]), row),
                  pl.BlockSpec((1, ROW_TILE, a_hg.shape[2]), row),
                  pl.BlockSpec(wa.shape, const2),
                  pl.BlockSpec(wb.shape, const2),
                  pl.BlockSpec((1, 1, 3, d), lambda i, r: (i, jnp.where(r * ROW_TILE >= lc, 1, 0), 0, 0))]
                 + _stream_specs(stream, lc),
        out_specs=pl.BlockSpec((1, ROW_TILE, d), row),
        out_shape=jax.ShapeDtypeStruct((b, t, d), F32),
        compiler_params=_cparams("arbitrary", "arbitrary"),
        name="even_out",
    )(a_gla, a_hg, wa, wb, mods, *parts)


def _rope(xf, cos, sin):
    w = xf.shape[1]
    reps = w // LANES
    cos_w = jnp.concatenate([cos] * reps, axis=1) if reps > 1 else cos
    sin_w = jnp.concatenate([sin] * reps, axis=1) if reps > 1 else sin
    lane = lax.broadcasted_iota(jnp.int32, xf.shape, 1)
    quarter = ATT_HEAD_DIM // 4
    partner = jnp.where(lane % (2 * quarter) < quarter,
                        pltpu.roll(xf, w - quarter, 1), pltpu.roll(xf, quarter, 1))
    return xf * cos_w + partner * sin_w


def _attn_kernel(q_ref, gate_ref, k_ref, v_ref, cq_ref, sq_ref, ck_ref, sk_ref, sink_ref, o_ref,
                 kr_scr, vt_scr, *, lc):
    n = pl.program_id(1)
    t_rows = k_ref.shape[1]
    length = t_rows - lc
    kv_heads = k_ref.shape[2] // ATT_HEAD_DIM
    band = 3 * ATT_BLOCK
    hd = ATT_HEAD_DIM

    @pl.when(n == 0)
    def _():
        def stage(i, carry):
            rows = pl.ds(pl.multiple_of(i * ROW_TILE, ROW_TILE), ROW_TILE)
            kf = k_ref[0, rows, :].astype(F32)
            lat = jnp.maximum(i * ROW_TILE - lc, 0)
            lrows = pl.ds(pl.multiple_of(lat, ROW_TILE), ROW_TILE)
            kr = jnp.where(i * ROW_TILE >= lc, _rope(kf, ck_ref[lrows, :], sk_ref[lrows, :]), kf).astype(BF16)
            vt = v_ref[0, rows, :].astype(F32).T.astype(BF16)
            for hk in range(kv_heads):
                kr_scr[hk, rows, :] = kr[:, hk * hd:(hk + 1) * hd]
                for half in range(ROW_TILE // ATT_BLOCK):
                    blk = i * (ROW_TILE // ATT_BLOCK) + half
                    vt_scr[hk, blk, 0:hd, :] = vt[hk * hd:(hk + 1) * hd, half * ATT_BLOCK:(half + 1) * ATT_BLOCK]
                    vt_scr[hk, blk, hd:2 * hd, :] = jnp.ones((hd, ATT_BLOCK), BF16)
            return carry
        lax.fori_loop(0, t_rows // ROW_TILE, stage, 0)

    start = pl.multiple_of(jnp.clip((n - 1) * ATT_BLOCK, 0, length - band), ATT_BLOCK)
    band_rows = pl.ds(pl.multiple_of(lc + start, ATT_BLOCK), band)
    band_blk = (lc + start) // ATT_BLOCK
    q = _rope(q_ref[0].astype(F32), cq_ref[...], sq_ref[...]) * (hd ** -0.5 * LOG2E)
    q_t = q.T.astype(BF16)
    kpos = start + lax.broadcasted_iota(jnp.int32, (band, ATT_BLOCK), 0)
    qpos = n * ATT_BLOCK + lax.broadcasted_iota(jnp.int32, (band, ATT_BLOCK), 1)
    valid = jnp.abs(qpos - kpos) <= WINDOW
    valid = jnp.concatenate([valid] * ATT_GROUP, axis=1)
    out_rows = []
    for hk in range(kv_heads):
        heads = [hk * ATT_GROUP + g for g in range(ATT_GROUP)]
        q4 = jnp.concatenate([q_t[h * hd:(h + 1) * hd, :] for h in heads], axis=1)
        sink = jnp.concatenate([sink_ref[h:h + 1, :] for h in heads], axis=1) * LOG2E
        s_c = _dot(kr_scr[hk, 0:lc, :], q4)
        s_b = jnp.where(valid, _dot(kr_scr[hk, band_rows, :], q4), NEG_INF)
        m = jnp.maximum(jnp.maximum(jnp.max(s_c, axis=0, keepdims=True),
                                    jnp.max(s_b, axis=0, keepdims=True)), sink)
        p = jnp.concatenate([jnp.exp2(s_c - m), jnp.exp2(s_b - m)], axis=0).astype(BF16)
        vt = jnp.concatenate([vt_scr[hk, c] for c in range(lc // ATT_BLOCK)]
                             + [vt_scr[hk, band_blk + c] for c in range(band // ATT_BLOCK)], axis=1)
        pv = _dot(vt, p)
        o_t = pv[0:hd] / (pv[hd:2 * hd] + jnp.exp2(sink - m))
        out_rows.extend(o_t[:, g * ATT_BLOCK:(g + 1) * ATT_BLOCK] for g in range(ATT_GROUP))
    o = jnp.concatenate(out_rows, axis=0).T
    o_ref[0] = (o * _silu(gate_ref[0].astype(F32))).astype(o_ref.dtype)


def _rope_tables(length, grid_w):
    quarter = ATT_HEAD_DIM // 4
    freqs = ROPE_BASE ** (-jnp.arange(quarter, dtype=F32) / quarter)
    pos = jnp.arange(length)
    ang_r = (pos // grid_w).astype(F32)[:, None] * freqs
    ang_c = (pos % grid_w).astype(F32)[:, None] * freqs
    cos = jnp.concatenate([jnp.cos(ang_r)] * 2 + [jnp.cos(ang_c)] * 2, axis=1)
    sin = jnp.concatenate([-jnp.sin(ang_r), jnp.sin(ang_r), -jnp.sin(ang_c), jnp.sin(ang_c)], axis=1)
    reps = LANES // ATT_HEAD_DIM
    return jnp.tile(cos, (1, reps)), jnp.tile(sin, (1, reps))


def _attention(p1, sink, lc, cols, d_model, grid_w):
    b, t, _ = p1.shape
    length = t - lc
    kv_w = d_model // ATT_GROUP
    assert length % ROW_TILE == 0 and length >= 3 * ATT_BLOCK and lc % ATT_BLOCK == 0
    cos, sin = _rope_tables(length, grid_w)
    sink_b = jnp.broadcast_to(sink.astype(F32)[:, None], (sink.shape[0], LANES))
    q_blk = lc // ATT_BLOCK
    qrow = lambda i, n: (i, q_blk + n, cols["q"] // d_model)
    grow = lambda i, n: (i, q_blk + n, cols["g_att"] // d_model)
    const2 = lambda i, n: (0, 0)
    return pl.pallas_call(
        functools.partial(_attn_kernel, lc=lc),
        grid=(b, length // ATT_BLOCK),
        in_specs=[pl.BlockSpec((1, ATT_BLOCK, d_model), qrow),
                  pl.BlockSpec((1, ATT_BLOCK, d_model), grow),
                  pl.BlockSpec((1, t, kv_w), lambda i, n: (i, 0, cols["k"] // kv_w)),
                  pl.BlockSpec((1, t, kv_w), lambda i, n: (i, 0, cols["v"] // kv_w)),
                  pl.BlockSpec((ATT_BLOCK, LANES), lambda i, n: (n, 0)),
                  pl.BlockSpec((ATT_BLOCK, LANES), lambda i, n: (n, 0)),
                  pl.BlockSpec((length, LANES), const2),
                  pl.BlockSpec((length, LANES), const2),
                  pl.BlockSpec(sink_b.shape, const2)],
        out_specs=pl.BlockSpec((1, ATT_BLOCK, d_model), lambda i, n: (i, n, 0)),
        out_shape=jax.ShapeDtypeStruct((b, length, d_model), BF16),
        scratch_shapes=[pltpu.VMEM((kv_w // ATT_HEAD_DIM, t, ATT_HEAD_DIM), BF16),
                        pltpu.VMEM((kv_w // ATT_HEAD_DIM, t // ATT_BLOCK, 2 * ATT_HEAD_DIM, ATT_BLOCK), BF16)],
        compiler_params=_cparams("arbitrary", "arbitrary"),
        name="window_attn",
    )(p1, p1, p1, p1, cos, sin, cos, sin, sink_b)


def _cmul(ar, ai, br, bi):
    return ar * br - ai * bi, ar * bi + ai * br


def _s5_operators(lam_re, lam_im, log_dt, b_re, b_im, c_re, c_im, d_skip):
    f32 = F32
    lam_re, lam_im, b_re, b_im, c_re, c_im = (a.astype(f32) for a in (lam_re, lam_im, b_re, b_im, c_re, c_im))
    n_groups, n_state, n_ch = b_re.shape
    n_blocks = n_groups // S5_GPB
    dt = jnp.exp(log_dt.astype(f32))[:, :, None]
    steps = jnp.arange(S5_CHUNK + 1, dtype=f32)[:, None, None, None]
    mag = jnp.exp(lam_re * dt * steps)
    pw_re, pw_im = mag * jnp.cos(lam_im * dt * steps), mag * jnp.sin(lam_im * dt * steps)
    a_re, a_im = pw_re[1], pw_im[1]
    inv = 1.0 / (lam_re * lam_re + lam_im * lam_im)
    co_re, co_im = _cmul(a_re - 1.0, a_im, lam_re * inv, -lam_im * inv)
    bb_re, bb_im = _cmul(co_re[..., None], co_im[..., None], b_re[None], b_im[None])
    e_re, e_im = _cmul(pw_re[..., None], pw_im[..., None], bb_re[None], bb_im[None])
    lag = S5_CHUNK
    k_lag = (jnp.einsum("gop,tdgpi->tdgoi", c_re, e_re[:lag], precision=HIGHEST)
             - jnp.einsum("gop,tdgpi->tdgoi", c_im, e_im[:lag], precision=HIGHEST))
    diag = k_lag[0, 0] + k_lag[0, 1] + d_skip.astype(f32)[:, :, None] * jnp.eye(n_ch, dtype=f32)
    by_lag = jnp.concatenate([k_lag[1:, 1][::-1], diag[None], k_lag[1:, 0]], axis=0)
    n_lags = 2 * lag - 1
    state_w = 4 * S5_GPB * n_state
    lag5 = by_lag.reshape(n_lags, n_blocks, S5_GPB, n_ch, n_ch)
    toep_c = lag5.transpose(1, 0, 2, 4, 3).reshape(n_blocks, n_lags * LANES, n_ch)
    n_re = jnp.stack([e_re[:lag][::-1, 0], e_re[:lag, 1]])
    n_im = jnp.stack([e_im[:lag][::-1, 0], e_im[:lag, 1]])
    n7 = jnp.stack([n_re, n_im], axis=2).reshape(2, lag, 2, n_blocks, S5_GPB, n_state, n_ch)
    state_c = n7.transpose(3, 1, 6, 0, 2, 4, 5).reshape(n_blocks, lag * n_ch, state_w)
    r_re = jnp.stack([pw_re[1:, 0], pw_re[1:, 1][::-1]])
    r_im = jnp.stack([pw_im[1:, 0], pw_im[1:, 1][::-1]])
    ca_re, ca_im = _cmul(c_re[None, None], c_im[None, None],
                         r_re[:, :, :, None, :], r_im[:, :, :, None, :])
    m7 = jnp.stack([ca_re, -ca_im], axis=1).reshape(2, 2, lag, n_blocks, S5_GPB, n_ch, n_state)
    read_c = m7.transpose(3, 0, 1, 4, 6, 2, 5).reshape(n_blocks, state_w, lag * n_ch)
    dec = jnp.stack([pw_re[lag], pw_im[lag]], axis=1)
    dec = dec.reshape(2, 2, n_blocks, S5_GPB * n_state).transpose(2, 0, 1, 3)
    dec = dec.reshape(n_blocks, 2, 1, 2 * S5_GPB * n_state)
    c = np.arange(S5_FOLD)
    src = (c // LANES) * n_ch + c % n_ch
    spread = jnp.asarray((np.arange(lag * n_ch)[:, None] == src[None, :]).astype(np.float32), BF16)
    return toep_c.astype(BF16), state_c.astype(BF16), read_c.astype(BF16), dec, spread


def _group_of(shape, axis, span):
    return (lax.broadcasted_iota(jnp.int32, shape, axis) // span) % S5_GPB


def _folded_rows(u_ref):
    return jnp.concatenate([u_ref[0, t] for t in range(S5_CHUNK)], axis=1)


def _s5_local_state_kernel(u_ref, sc_ref, spread_ref, o_ref, w_scr):
    @pl.when(pl.program_id(1) == 0)
    def _():
        n_state = sc_ref.shape[2] // (4 * S5_GPB)
        for m in range(S5_FOLD // ROW_TILE):
            rows = slice(m * ROW_TILE, (m + 1) * ROW_TILE)
            full = _dot_tn(spread_ref[:, rows], sc_ref[0])
            keep = _group_of(full.shape, 0, S5_GROUP_CH) == _group_of(full.shape, 1, n_state)
            w_scr[rows, :] = jnp.where(keep, full, 0.0).astype(BF16)
    o_ref[0] = _dot(_folded_rows(u_ref), w_scr[...])


def _s5_state_kernel(xl_ref, dec_ref, xp_ref, *, batch, n_ctx_chunks):
    d = pl.program_id(1)
    n_chunks = xl_ref.shape[1] // batch
    half = xl_ref.shape[2] // 2
    a_re, a_im = dec_ref[0, 0, :, :half], dec_ref[0, 0, :, half:]

    def step(i, state):
        s_re, s_im = state
        c_bwd = jnp.where(i < n_ctx_chunks, n_ctx_chunks - 1 - i, n_chunks - 1 - (i - n_ctx_chunks))
        c = jnp.where(d == 0, i, c_bwd)
        rows = pl.ds(pl.multiple_of(c * batch, batch), batch)
        xp_ref[0, rows, :] = jnp.concatenate([s_re, s_im], axis=1).astype(xp_ref.dtype)
        xl = xl_ref[0, rows, :]
        return (a_re * s_re - a_im * s_im + xl[:, :half], a_re * s_im + a_im * s_re + xl[:, half:])

    zero = jnp.zeros((batch, half), F32)
    lax.fori_loop(0, n_chunks, step, (zero, zero))


def _s5_out_kernel(u_ref, xp_ref, tc_ref, rc_ref, spread_ref, o_ref, lag_scr, toep_scr, read_scr):
    @pl.when(pl.program_id(1) == 0)
    def _():
        n_state = rc_ref.shape[1] // (4 * S5_GPB)
        n_ch = tc_ref.shape[2]
        full = _dot(tc_ref[0], spread_ref[0:n_ch, 0:LANES])
        keep = _group_of(full.shape, 0, S5_GROUP_CH) == _group_of(full.shape, 1, S5_GROUP_CH)
        lag_scr[...] = jnp.where(keep, full, 0.0).astype(BF16)
        for t_in in range(S5_CHUNK):
            for t_out in range(S5_CHUNK):
                lag = t_out - t_in + S5_CHUNK - 1
                toep_scr[t_in * LANES:(t_in + 1) * LANES, t_out * LANES:(t_out + 1) * LANES] = (
                    lag_scr[lag * LANES:(lag + 1) * LANES, :])
        for m in range(S5_FOLD // ROW_TILE):
            cols = slice(m * ROW_TILE, (m + 1) * ROW_TILE)
            full = _dot(rc_ref[0], spread_ref[:, cols])
            keep = _group_of(full.shape, 0, n_state) == _group_of(full.shape, 1, S5_GROUP_CH)
            read_scr[:, cols] = jnp.where(keep, full, 0.0).astype(BF16)
    y = _dot(_folded_rows(u_ref), toep_scr[...]) + _dot(xp_ref[0], read_scr[...])
    for t in range(S5_CHUNK):
        o_ref[0, t] = y[:, t * LANES:(t + 1) * LANES]


def _s5(u, lc, ops):
    toep_c, state_c, read_c, dec, spread = ops
    b, t, w = u.shape
    n_blocks = w // LANES
    n_chunks, n_ctx_chunks = t // S5_CHUNK, lc // S5_CHUNK
    rows = n_chunks * b
    state_w = state_c.shape[2]
    assert b % 16 == 0 and rows % ROW_TILE == 0 and (n_ctx_chunks * b) % ROW_TILE == 0
    u2 = u.reshape(b, n_chunks, S5_CHUNK, n_blocks, LANES).transpose(3, 2, 1, 0, 4)
    u2 = u2.reshape(n_blocks, S5_CHUNK, rows, LANES)
    blk_row = lambda g, r: (g, r, 0)
    blk_w = lambda g, r: (g, 0, 0)
    const2 = lambda g, r: (0, 0)
    x_loc = pl.pallas_call(
        _s5_local_state_kernel,
        grid=(n_blocks, rows // ROW_TILE),
        in_specs=[pl.BlockSpec((1, S5_CHUNK, ROW_TILE, LANES), lambda g, r: (g, 0, r, 0)),
                  pl.BlockSpec((1,) + state_c.shape[1:], blk_w),
                  pl.BlockSpec(spread.shape, const2)],
        out_specs=pl.BlockSpec((1, ROW_TILE, state_w), blk_row),
        out_shape=jax.ShapeDtypeStruct((n_blocks, rows, state_w), F32),
        scratch_shapes=[pltpu.VMEM((S5_FOLD, state_w), BF16)],
        compiler_params=_cparams("arbitrary", "arbitrary"),
        name="s5_local_state",
    )(u2, state_c, spread)
    x_prev = pl.pallas_call(
        functools.partial(_s5_state_kernel, batch=b, n_ctx_chunks=n_ctx_chunks),
        grid=(n_blocks, 2),
        in_specs=[pl.BlockSpec((1, rows, state_w // 2), lambda g, d: (g, 0, d)),
                  pl.BlockSpec((1, 1, 1, state_w // 2), lambda g, d: (g, d, 0, 0))],
        out_specs=pl.BlockSpec((1, rows, state_w // 2), lambda g, d: (g, 0, d)),
        out_shape=jax.ShapeDtypeStruct((n_blocks, rows, state_w), BF16),
        compiler_params=_cparams("arbitrary", "arbitrary"),
        name="s5_state_scan",
    )(x_loc, dec)
    ctx_tiles = n_ctx_chunks * b // ROW_TILE
    lat_rows = rows - n_ctx_chunks * b
    lat_row = lambda g, r: (g, ctx_tiles + r, 0)
    y2 = pl.pallas_call(
        _s5_out_kernel,
        grid=(n_blocks, lat_rows // ROW_TILE),
        in_specs=[pl.BlockSpec((1, S5_CHUNK, ROW_TILE, LANES), lambda g, r: (g, 0, ctx_tiles + r, 0)),
                  pl.BlockSpec((1, ROW_TILE, state_w), lat_row),
                  pl.BlockSpec((1,) + toep_c.shape[1:], blk_w),
                  pl.BlockSpec((1,) + read_c.shape[1:], blk_w),
                  pl.BlockSpec(spread.shape, const2)],
        out_specs=pl.BlockSpec((1, S5_CHUNK, ROW_TILE, LANES), lambda g, r: (g, 0, r, 0)),
        out_shape=jax.ShapeDtypeStruct((n_blocks, S5_CHUNK, lat_rows, LANES), F32),
        scratch_shapes=[pltpu.VMEM(toep_c.shape[1:2] + (LANES,), BF16),
                        pltpu.VMEM((S5_FOLD, S5_FOLD), BF16), pltpu.VMEM((state_w, S5_FOLD), BF16)],
        compiler_params=_cparams("arbitrary", "arbitrary"),
        name="s5_chunk_out",
    )(u2, x_prev, toep_c, read_c, spread)
    y = y2.reshape(n_blocks, S5_CHUNK, n_chunks - n_ctx_chunks, b, LANES).transpose(0, 3, 2, 1, 4)
    return y.reshape(n_blocks, b, t - lc, LANES)


def _odd_out_kernel(att_ref, y_ref, gs_ref, glu_ref, wa_ref, wb_ref, x_ref, mod_ref, fg_ref, o_ref):
    y_s5 = jnp.concatenate([y_ref[g, 0] for g in range(y_ref.shape[0])], axis=1)
    width = y_s5.shape[1]
    z = jax.nn.gelu(y_s5).astype(BF16)
    ab = _dot(z, glu_ref[...])
    s5 = ab[:, :width] * jax.nn.sigmoid(ab[:, width:]) * _silu(gs_ref[0].astype(F32))
    y = _dot(att_ref[0], wa_ref[...]) + _dot(s5.astype(BF16), wb_ref[...])
    xo = x_ref[0] + mod_ref[0, 0, 2:3, :] * y
    ms = jnp.mean(xo * xo, axis=-1, keepdims=True)
    o_ref[0] = xo * lax.rsqrt(ms + EPS) * fg_ref[...]


def _odd_out(att, y_s5, p1, glu_w, w_out, xs, mods, final_g, lc, cols):
    b, length, d = att.shape
    wa, wb = w_out[:d].astype(BF16), w_out[d:].astype(BF16)
    lat = lc // ROW_TILE
    row = lambda i, r: (i, r, 0)
    const2 = lambda i, r: (0, 0)
    return pl.pallas_call(
        _odd_out_kernel,
        grid=(b, length // ROW_TILE),
        in_specs=[pl.BlockSpec((1, ROW_TILE, d), row),
                  pl.BlockSpec((y_s5.shape[0], 1, ROW_TILE, LANES), lambda i, r: (0, i, r, 0)),
                  pl.BlockSpec((1, ROW_TILE, d), lambda i, r: (i, lat + r, cols["g_s5"] // d)),
                  pl.BlockSpec(glu_w.shape, const2),
                  pl.BlockSpec(wa.shape, const2),
                  pl.BlockSpec(wb.shape, const2),
                  pl.BlockSpec((1, ROW_TILE, d), lambda i, r: (i, lat + r, 0)),
                  pl.BlockSpec((1, 1, 3, d), lambda i, r: (i, 1, 0, 0)),
                  pl.BlockSpec((1, d), const2)],
        out_specs=pl.BlockSpec((1, ROW_TILE, d), row),
        out_shape=jax.ShapeDtypeStruct((b, length, d), F32),
        compiler_params=_cparams("arbitrary", "arbitrary"),
        name="odd_out",
    )(att, y_s5, p1, glu_w.astype(BF16), wa, wb, xs, mods, final_g.reshape(1, d))


def _offsets(names_widths):
    cols, off = {}, 0
    for name, width in names_widths:
        cols[name] = off
        off += width
    return cols, off


def kernel(x, c, ctx, c_ctx, ada_w, ada_b, norm_g, final_norm_g, ev_w_in, ev_w_out, gla_gk_w, gla_gk_b,
           gla_norm_g, hgrn_lb_raw, hgrn_norm_g, od_w_in, od_w_out, attn_sink, s5_lambda_re, s5_lambda_im,
           s5_log_dt, s5_b_re, s5_b_im, s5_c_re, s5_c_im, s5_d, s5_glu_w):
    b, length, d = x.shape
    lc = ctx.shape[1]
    assert ada_w.shape[0] == 2 and ev_w_in.shape[0] == 1 and od_w_in.shape[0] == 1
    grid_w = 64
    xs = (ctx, x)

    pad = (-(b + 1)) % 8
    cc = jnp.concatenate([c, c_ctx[None], jnp.zeros((pad, d), c.dtype)], axis=0)
    mods_all = _ada_mods(cc, ada_w, ada_b)

    def layer_mods(layer):
        lat = mods_all[layer, :b].reshape(b, 1, 3, d)
        cx = jnp.broadcast_to(mods_all[layer, b].reshape(1, 1, 3, d), (b, 1, 3, d))
        return jnp.concatenate([cx, lat], axis=1)

    mods0 = layer_mods(0)
    half = d // 2
    w0 = ev_w_in[0]
    src, _ = _offsets([("aq", half), ("ak", half), ("av", d), ("alr", 2 * GLA_GATE_RANK), ("agate", d),
                       ("bq", d), ("bf", 2 * d), ("bi", d), ("bgate", d)])
    bf_names = [("aq", half), ("ak", half), ("av", d), ("agate", d), ("bq", d), ("bi", d), ("bgate", d)]
    cols_bf, _ = _offsets(bf_names)
    w0_bf = jnp.concatenate([w0[:, src[n]:src[n] + wd] for n, wd in bf_names], axis=1).astype(BF16)
    lr_pad = ROW_TILE - 2 * GLA_GATE_RANK
    cols_f32, _ = _offsets([("bf0", d), ("bf1", d), ("alr", ROW_TILE)])
    w0_f32 = jnp.concatenate([w0[:, src["bf"]:src["bf"] + 2 * d],
                              w0[:, src["alr"]:src["alr"] + 2 * GLA_GATE_RANK],
                              jnp.zeros((d, lr_pad), w0.dtype)], axis=1).astype(BF16)
    p_bf, p_f32 = _inproj(xs, mods0, norm_g[0], [w0_bf, w0_f32], [BF16, F32], lc)
    lb_all = jnp.cumsum(jax.nn.softmax(hgrn_lb_raw.astype(F32), axis=1), axis=1)
    cols0 = dict(cols_bf, **cols_f32)
    a_gla, a_hg = _even_mixers(p_bf, p_f32, gla_gk_w[0], gla_gk_b[0], gla_norm_g[0], lb_all[:, 0],
                               hgrn_norm_g[0], lc, cols0, d)
    xs1 = _even_out(a_gla, a_hg, ev_w_out[0], xs, mods0, lc)

    mods1 = layer_mods(1)
    kv_w = d // ATT_GROUP
    w1 = od_w_in[0]
    src1, _ = _offsets([("q", d), ("k", kv_w), ("v", kv_w), ("g_att", d), ("u", d), ("g_s5", d)])
    names1 = [("q", d), ("g_att", d), ("g_s5", d), ("k", kv_w), ("v", kv_w)]
    cols1, _ = _offsets(names1)
    w1_bf = jnp.concatenate([w1[:, src1[n]:src1[n] + wd] for n, wd in names1], axis=1).astype(BF16)
    w1_u = w1[:, src1["u"]:src1["u"] + d].astype(BF16)
    p1, u = _inproj(xs1, mods1, norm_g[1], [w1_bf, w1_u], [BF16, BF16], lc)
    att = _attention(p1, attn_sink[0], lc, cols1, d, grid_w)
    ops = _s5_operators(s5_lambda_re[0], s5_lambda_im[0], s5_log_dt[0], s5_b_re[0], s5_b_im[0],
                        s5_c_re[0], s5_c_im[0], s5_d[0])
    y_s5 = _s5(u, lc, ops)
    return _odd_out(att, y_s5, p1, s5_glu_w[0], od_w_out[0], xs1, mods1, final_norm_g, lc, cols1)
```
